```python
import math
import jax, jax.numpy as jnp
from jax import lax
import numpy as np

D_MODEL = 1024
BATCH = 4
SEQ = 8192
DEPTH = 1
DEC_BATCH = 128
DEC_SEQ = 1
PAST_LEN = 8192
PAGE_SIZE = 128

D_SSM = D_MODEL // 2
SSM_GROUP = 16
N_GROUPS = D_SSM // SSM_GROUP
STATE_DIM = 64
N_HEADS = 8
HEAD_DIM = 64
D_ATT = N_HEADS * HEAD_DIM
D_FF = ((8 * D_MODEL // 3 + 127) // 128) * 128
CONV_W = 3
PLE_DIM = 256
Q_BLOCK = 128
ALPHA = (2 * DEPTH) ** 0.25
BETA = (8 * DEPTH) ** -0.25
LN_EPS = 1e-5
NEG_INF = -1e30
IN_SPLITS = (D_SSM, D_ATT, D_ATT, D_ATT, N_HEADS, D_MODEL, D_MODEL)
N_IN = sum(IN_SPLITS)

kernel_name = 'hybrid_s5_fox_convffn_step'


def layer_norm(x, g, b):
    xf = x.astype(jnp.float32)
    mu = xf.mean(-1, keepdims=True)
    var = jnp.square(xf - mu).mean(-1, keepdims=True)
    return ((xf - mu) * lax.rsqrt(var + LN_EPS) * g + b).astype(x.dtype)


def split_in(z):
    idx = np.cumsum(IN_SPLITS)[:-1].tolist()
    return jnp.split(z, idx, axis=-1)


def complex_affine_combine(e1, e2):
    ar1, ai1, br1, bi1 = e1
    ar2, ai2, br2, bi2 = e2
    ar = ar1 * ar2 - ai1 * ai2
    ai = ar1 * ai2 + ai1 * ar2
    br = ar2 * br1 - ai2 * bi1 + br2
    bi = ar2 * bi1 + ai2 * br1 + bi2
    return ar, ai, br, bi


def ssm_discretize(a_re, a_im, log_dt, b_re, b_im):
    f32 = jnp.float32
    a_re = a_re.astype(f32)
    a_im = a_im.astype(f32)
    dt = jnp.exp(log_dt.astype(f32))[:, None]
    mag = jnp.exp(dt * a_re)
    ang = dt * a_im
    e_re = mag * jnp.cos(ang)
    e_im = mag * jnp.sin(ang)
    nr = e_re - 1.0
    ni = e_im
    den = a_re * a_re + a_im * a_im
    z_re = ((nr * a_re + ni * a_im) / den)[..., None]
    z_im = ((ni * a_re - nr * a_im) / den)[..., None]
    b_re = b_re.astype(f32)
    b_im = b_im.astype(f32)
    bb_re = z_re * b_re - z_im * b_im
    bb_im = z_re * b_im + z_im * b_re
    return e_re, e_im, bb_re, bb_im


def ssm_mixer(u, h0_re, h0_im, lw):
    f32 = jnp.float32
    bsz, L, _ = u.shape
    e_re, e_im, bb_re, bb_im = ssm_discretize(lw['ssm_a_re'], lw['ssm_a_im'], lw['ssm_log_dt'],
                                              lw['ssm_b_re'], lw['ssm_b_im'])
    uf = u.astype(f32)
    ug = uf.reshape(bsz, L, N_GROUPS, SSM_GROUP)
    bu_re = jnp.einsum('blgc,gpc->blgp', ug, bb_re)
    bu_im = jnp.einsum('blgc,gpc->blgp', ug, bb_im)
    h0r = h0_re.astype(f32)
    h0i = h0_im.astype(f32)
    bu_re = bu_re.at[:, 0].add(e_re * h0r - e_im * h0i)
    bu_im = bu_im.at[:, 0].add(e_re * h0i + e_im * h0r)
    a_re = jnp.broadcast_to(e_re, bu_re.shape)
    a_im = jnp.broadcast_to(e_im, bu_im.shape)
    _, _, s_re, s_im = lax.associative_scan(complex_affine_combine, (a_re, a_im, bu_re, bu_im), axis=1)
    c_re = lw['ssm_c_re'].astype(f32)
    c_im = lw['ssm_c_im'].astype(f32)
    y = jnp.einsum('blgp,gcp->blgc', s_re, c_re) - jnp.einsum('blgp,gcp->blgc', s_im, c_im)
    y = y.reshape(bsz, L, D_SSM) + lw['ssm_d'].astype(f32) * uf
    y = jax.nn.gelu(y)
    y = y * jax.nn.sigmoid(y @ lw['w_glu'].astype(f32) + lw['b_glu'].astype(f32))
    return y.astype(u.dtype), s_re[:, -1], s_im[:, -1]


def forgetting_attention_prompt(q, k, v, logf):
    f32 = jnp.float32
    bsz, L, H, Dh = q.shape
    nb = L // Q_BLOCK
    scale = HEAD_DIM ** -0.5
    Ft = jnp.cumsum(logf, axis=1).transpose(0, 2, 1)
    kf = k.astype(f32)
    vf = v.astype(f32)
    qb = q.astype(f32).reshape(bsz, nb, Q_BLOCK, H, Dh).transpose(1, 0, 2, 3, 4)
    Fb = Ft.reshape(bsz, H, nb, Q_BLOCK).transpose(2, 0, 1, 3)
    starts = jnp.arange(nb, dtype=jnp.int32) * Q_BLOCK
    k_pos = jnp.arange(L, dtype=jnp.int32)

    def one_block(args):
        qi, Fi, s0 = args
        logits = jnp.einsum('bqhd,bkhd->bhqk', qi, kf) * scale + (Fi[..., None] - Ft[:, :, None, :])
        q_pos = s0 + jnp.arange(Q_BLOCK, dtype=jnp.int32)
        mask = k_pos[None, :] <= q_pos[:, None]
        logits = jnp.where(mask, logits, NEG_INF)
        p = jax.nn.softmax(logits, axis=-1)
        return jnp.einsum('bhqk,bkhd->bqhd', p, vf)

    out = lax.map(one_block, (qb, Fb, starts))
    return out.transpose(1, 0, 2, 3, 4).reshape(bsz, L, H * Dh).astype(q.dtype)


def forgetting_attention_sample(q, k, v, logf, cache_k, cache_v, cache_logf, page_table, layer):
    f32 = jnp.float32
    bsz, S, H, Dh = q.shape
    n_pages = page_table.shape[1]
    past = n_pages * cache_k.shape[2]
    scale = HEAD_DIM ** -0.5
    k_past = cache_k[layer, page_table].reshape(bsz, past, H, Dh).astype(f32)
    v_past = cache_v[layer, page_table].reshape(bsz, past, H, Dh).astype(f32)
    lf_past = cache_logf[layer, page_table].reshape(bsz, past, H).astype(f32)
    F = jnp.cumsum(jnp.concatenate([lf_past, logf], axis=1), axis=1).transpose(0, 2, 1)
    Fq = F[:, :, past:]
    Fk_past = F[:, :, :past]
    qf = q.astype(f32)
    kf = k.astype(f32)
    vf = v.astype(f32)
    l_past = jnp.einsum('bqhd,bkhd->bhqk', qf, k_past) * scale + (Fq[..., None] - Fk_past[:, :, None, :])
    l_new = jnp.einsum('bqhd,bkhd->bhqk', qf, kf) * scale + (Fq[..., None] - Fq[:, :, None, :])
    causal = jnp.tril(jnp.ones((S, S), dtype=bool))
    l_new = jnp.where(causal, l_new, NEG_INF)
    p = jax.nn.softmax(jnp.concatenate([l_past, l_new], axis=-1), axis=-1)
    out = (jnp.einsum('bhqk,bkhd->bqhd', p[..., :past], v_past)
           + jnp.einsum('bhqk,bkhd->bqhd', p[..., past:], vf))
    return out.reshape(bsz, S, H * Dh).astype(q.dtype)


def conv_ffn(x, conv_buf, lw):
    up = x @ lw['w_up']
    a, b = jnp.split(up, 2, axis=-1)
    L = a.shape[1]
    ext = jnp.concatenate([conv_buf.astype(a.dtype), a], axis=1)
    w = lw['conv_w']
    c = lw['conv_b'] + w[0] * ext[:, 0:L]
    for j in range(1, CONV_W):
        c = c + w[j] * ext[:, j:j + L]
    h = jax.nn.gelu(c) * b
    return h @ lw['w_down'], ext[:, -(CONV_W - 1):]


def trunk_layer(x, p_l, h0_re, h0_im, conv_buf, attend, lw):
    bsz, L, _ = x.shape
    z = x @ lw['w_in']
    u, q, k, v, f_logit, g_ssm, g_att = split_in(z)
    q = q.reshape(bsz, L, N_HEADS, HEAD_DIM)
    k = k.reshape(bsz, L, N_HEADS, HEAD_DIM)
    v = v.reshape(bsz, L, N_HEADS, HEAD_DIM)
    logf = jax.nn.log_sigmoid((f_logit + lw['b_f']).astype(jnp.float32))
    y_ssm, hT_re, hT_im = ssm_mixer(u, h0_re, h0_im, lw)
    y_att = attend(q, k, v, logf)
    mixed = (jax.nn.sigmoid(g_ssm) * (y_ssm @ lw['w_br_ssm'])
             + jax.nn.sigmoid(g_att) * (y_att @ lw['w_br_att']))
    x = layer_norm(ALPHA * x + mixed @ lw['w_o'], lw['ln1_g'], lw['ln1_b'])
    f, conv_new = conv_ffn(x, conv_buf, lw)
    x = layer_norm(ALPHA * x + f, lw['ln2_g'], lw['ln2_b'])
    x = x + jax.nn.sigmoid(x @ lw['w_ple_gate'] + lw['b_ple_gate']) * (p_l @ lw['w_ple'])
    return x, (k, v, logf, hT_re, hT_im, conv_new)


def setup_inputs(seed: int = 0) -> dict:
    key = jax.random.key(seed)
    ks = iter(jax.random.split(key, 48))
    f32 = jnp.float32

    def nrm(shape, scale=1.0):
        return jax.random.normal(next(ks), shape, f32) * scale

    n_pages = PAST_LEN // PAGE_SIZE
    n_phys = (DEC_BATCH * n_pages * 5) // 4
    x_prompt = nrm((BATCH, SEQ, D_MODEL))
    x_sample = nrm((DEC_BATCH, DEC_SEQ, D_MODEL))
    p_prompt = nrm((DEPTH, BATCH, SEQ, PLE_DIM))
    p_sample = nrm((DEPTH, DEC_BATCH, DEC_SEQ, PLE_DIM))
    cache_k = nrm((DEPTH, n_phys, PAGE_SIZE, N_HEADS, HEAD_DIM))
    cache_v = nrm((DEPTH, n_phys, PAGE_SIZE, N_HEADS, HEAD_DIM))
    cache_logf = jax.nn.log_sigmoid(4.0 + nrm((DEPTH, n_phys, PAGE_SIZE, N_HEADS)))
    perm = jax.random.permutation(next(ks), n_phys)
    page_table = perm[:DEC_BATCH * n_pages].reshape(DEC_BATCH, n_pages).astype(jnp.int32)
    state_ssm_re = nrm((DEPTH, DEC_BATCH, N_GROUPS, STATE_DIM), 0.5)
    state_ssm_im = nrm((DEPTH, DEC_BATCH, N_GROUPS, STATE_DIM), 0.5)
    state_conv = nrm((DEPTH, DEC_BATCH, CONV_W - 1, D_FF))

    w_in = nrm((DEPTH, D_MODEL, N_IN), D_MODEL ** -0.5)
    b_f = jnp.linspace(2.0, 6.0, N_HEADS, dtype=f32)[None, :] + nrm((DEPTH, N_HEADS), 0.1)
    ssm_a_re = -0.5 + nrm((DEPTH, N_GROUPS, STATE_DIM), 0.01)
    ssm_a_im = jnp.pi * jnp.arange(STATE_DIM, dtype=f32) + nrm((DEPTH, N_GROUPS, STATE_DIM), 0.01)
    ssm_log_dt = jax.random.uniform(next(ks), (DEPTH, N_GROUPS), f32,
                                    minval=math.log(1e-3), maxval=math.log(1e-1))
    ssm_b_re = nrm((DEPTH, N_GROUPS, STATE_DIM, SSM_GROUP), (2 * SSM_GROUP) ** -0.5)
    ssm_b_im = nrm((DEPTH, N_GROUPS, STATE_DIM, SSM_GROUP), (2 * SSM_GROUP) ** -0.5)
    ssm_c_re = nrm((DEPTH, N_GROUPS, SSM_GROUP, STATE_DIM), (2 * STATE_DIM) ** -0.5)
    ssm_c_im = nrm((DEPTH, N_GROUPS, SSM_GROUP, STATE_DIM), (2 * STATE_DIM) ** -0.5)
    ssm_d = nrm((DEPTH, D_SSM))
    w_glu = nrm((DEPTH, D_SSM, D_SSM), D_SSM ** -0.5)
    b_glu = nrm((DEPTH, D_SSM), 0.01)
    w_br_ssm = nrm((DEPTH, D_SSM, D_MODEL), D_SSM ** -0.5)
    w_br_att = nrm((DEPTH, D_ATT, D_MODEL), D_ATT ** -0.5)
    w_o = nrm((DEPTH, D_MODEL, D_MODEL), BETA * D_MODEL ** -0.5)
    ln1_g = 1.0 + nrm((DEPTH, D_MODEL), 0.01)
    ln1_b = nrm((DEPTH, D_MODEL), 0.01)
    w_up = nrm((DEPTH, D_MODEL, 2 * D_FF), D_MODEL ** -0.5)
    conv_w = nrm((DEPTH, CONV_W, D_FF), CONV_W ** -0.5)
    conv_b = nrm((DEPTH, D_FF), 0.01)
    w_down = nrm((DEPTH, D_FF, D_MODEL), BETA * D_FF ** -0.5)
    ln2_g = 1.0 + nrm((DEPTH, D_MODEL), 0.01)
    ln2_b = nrm((DEPTH, D_MODEL), 0.01)
    w_ple_gate = nrm((DEPTH, D_MODEL, D_MODEL), D_MODEL ** -0.5)
    b_ple_gate = nrm((DEPTH, D_MODEL), 0.01)
    w_ple = nrm((DEPTH, PLE_DIM, D_MODEL), PLE_DIM ** -0.5)
    return dict(x_prompt=x_prompt, x_sample=x_sample, p_prompt=p_prompt, p_sample=p_sample,
                cache_k=cache_k, cache_v=cache_v, cache_logf=cache_logf, page_table=page_table,
                state_ssm_re=state_ssm_re, state_ssm_im=state_ssm_im, state_conv=state_conv,
                w_in=w_in, b_f=b_f, ssm_a_re=ssm_a_re, ssm_a_im=ssm_a_im, ssm_log_dt=ssm_log_dt,
                ssm_b_re=ssm_b_re, ssm_b_im=ssm_b_im, ssm_c_re=ssm_c_re, ssm_c_im=ssm_c_im,
                ssm_d=ssm_d, w_glu=w_glu, b_glu=b_glu, w_br_ssm=w_br_ssm, w_br_att=w_br_att,
                w_o=w_o, ln1_g=ln1_g, ln1_b=ln1_b, w_up=w_up, conv_w=conv_w, conv_b=conv_b,
                w_down=w_down, ln2_g=ln2_g, ln2_b=ln2_b, w_ple_gate=w_ple_gate,
                b_ple_gate=b_ple_gate, w_ple=w_ple)


def reference(x_prompt, x_sample, p_prompt, p_sample, cache_k, cache_v, cache_logf, page_table,
              state_ssm_re, state_ssm_im, state_conv, w_in, b_f, ssm_a_re, ssm_a_im, ssm_log_dt,
              ssm_b_re, ssm_b_im, ssm_c_re, ssm_c_im, ssm_d, w_glu, b_glu, w_br_ssm, w_br_att,
              w_o, ln1_g, ln1_b, w_up, conv_w, conv_b, w_down, ln2_g, ln2_b, w_ple_gate,
              b_ple_gate, w_ple):
    hp = x_prompt
    hs = x_sample
    bp = x_prompt.shape[0]
    h0_prompt = jnp.zeros((bp, N_GROUPS, STATE_DIM), jnp.float32)
    conv0_prompt = jnp.zeros((bp, CONV_W - 1, D_FF), x_prompt.dtype)
    new_p = []
    new_s = []
    for l in range(DEPTH):
        lw = dict(w_in=w_in[l], b_f=b_f[l], ssm_a_re=ssm_a_re[l], ssm_a_im=ssm_a_im[l],
                  ssm_log_dt=ssm_log_dt[l], ssm_b_re=ssm_b_re[l], ssm_b_im=ssm_b_im[l],
                  ssm_c_re=ssm_c_re[l], ssm_c_im=ssm_c_im[l], ssm_d=ssm_d[l], w_glu=w_glu[l],
                  b_glu=b_glu[l], w_br_ssm=w_br_ssm[l], w_br_att=w_br_att[l], w_o=w_o[l],
                  ln1_g=ln1_g[l], ln1_b=ln1_b[l], w_up=w_up[l], conv_w=conv_w[l], conv_b=conv_b[l],
                  w_down=w_down[l], ln2_g=ln2_g[l], ln2_b=ln2_b[l], w_ple_gate=w_ple_gate[l],
                  b_ple_gate=b_ple_gate[l], w_ple=w_ple[l])
        hp, st_p = trunk_layer(hp, p_prompt[l], h0_prompt, h0_prompt, conv0_prompt,
                               forgetting_attention_prompt, lw)
        attend_s = lambda q, k, v, lf, layer=l: forgetting_attention_sample(
            q, k, v, lf, cache_k, cache_v, cache_logf, page_table, layer)
        hs, st_s = trunk_layer(hs, p_sample[l], state_ssm_re[l], state_ssm_im[l], state_conv[l],
                               attend_s, lw)
        new_p.append(st_p)
        new_s.append(st_s)

    def stack(lst, i):
        return jnp.stack([s[i] for s in lst])

    return (hp, hs,
            stack(new_p, 0), stack(new_p, 1), stack(new_p, 2), stack(new_p, 3), stack(new_p, 4), stack(new_p, 5),
            stack(new_s, 0), stack(new_s, 1), stack(new_s, 2), stack(new_s, 3), stack(new_s, 4), stack(new_s, 5))
```

```python
import functools

import jax
import jax.numpy as jnp
from jax import lax
from jax.experimental import pallas as pl
from jax.experimental.pallas import tpu as pltpu

F32 = jnp.float32
BF16 = jnp.bfloat16

D_MODEL = 1024
D_SSM = 512
SSM_GROUP = 16
N_GROUPS = 32
STATE_DIM = 64
N_STATE = N_GROUPS * STATE_DIM
N_HEADS = 8
HEAD_DIM = 64
D_ATT = 512
D_FF = 2816
CONV_W = 3
DEPTH = 1
ALPHA = (2 * DEPTH) ** 0.25
LN_EPS = 1e-5
NEG_BIG = -1e30
LANES = 128
SUBLANES = 8
KAUG = 128
VMEM_LIMIT = 56 * 1024 * 1024


def _dot(a, b):
    return jnp.dot(a, b, preferred_element_type=F32)


def _dot_nt(a, b):
    return lax.dot_general(a, b, (((1,), (1,)), ((), ())), preferred_element_type=F32)


def _log_sigmoid(x):
    return jnp.minimum(x, 0.0) - jnp.log1p(jnp.exp(-jnp.abs(x)))


def _split3(x):
    hi = x.astype(BF16)
    r1 = x - hi.astype(F32)
    mid = r1.astype(BF16)
    lo = (r1 - mid.astype(F32)).astype(BF16)
    return hi, mid, lo


def _layer_norm(t, g, b):
    mu = jnp.mean(t, axis=-1, keepdims=True)
    c = t - mu
    var = jnp.mean(c * c, axis=-1, keepdims=True)
    return c * lax.rsqrt(var + LN_EPS) * g + b


def _params(sem):
    return pltpu.CompilerParams(dimension_semantics=sem, vmem_limit_bytes=VMEM_LIMIT)


def _const_spec(shape):
    nd = len(shape)
    return pl.BlockSpec(shape, lambda *_: (0,) * nd)


def _inproj_prompt_kernel(x_ref, w1_ref, wt_ref, bfr_ref, bfc_ref, tri_ref, place_ref,
                          u_ref, kaug_ref, qaug_ref, kt_ref, vt_ref, vtb_ref, lft_ref, carry_ref, *, tm):
    @pl.when(pl.program_id(1) == 0)
    def _():
        carry_ref[...] = jnp.zeros_like(carry_ref)

    xb = x_ref[0].astype(BF16)
    z1 = _dot(xb, w1_ref[...])
    u_ref[0] = z1[:, :D_SSM]
    kpad = z1[:, D_SSM:D_SSM + N_HEADS * KAUG]
    f_logit = z1[:, D_SSM + N_HEADS * KAUG:]

    lane = lax.broadcasted_iota(jnp.int32, (tm, LANES), 1)
    lf = jnp.where(lane < N_HEADS, _log_sigmoid(f_logit + bfr_ref[...]), 0.0)
    hi, mid, lo = _split3(lf)
    cum = _dot(tri_ref[...], jnp.concatenate([hi, mid, lo], axis=1))
    fcum = (cum[:, :LANES] + cum[:, LANES:2 * LANES]) + cum[:, 2 * LANES:] + carry_ref[0:1, :]
    carry_ref[...] = jnp.broadcast_to(fcum[tm - 1:tm, :], carry_ref.shape)
    nhi, nmid, nlo = _split3(-fcum)
    kaug = kpad + _dot(jnp.concatenate([nhi, nmid, nlo], axis=1), place_ref[...])
    kaug = kaug.astype(BF16)
    for h in range(N_HEADS):
        kaug_ref[0, h] = kaug[:, h * KAUG:(h + 1) * KAUG]

    zt = _dot_nt(wt_ref[...], xb)
    row = lax.broadcasted_iota(jnp.int32, (KAUG - HEAD_DIM, tm), 0)
    ones_rows = jnp.where(row < 3, 1.0, 0.0).astype(BF16)
    for h in range(N_HEADS):
        qaug_ref[0, h, 0:HEAD_DIM, :] = zt[h * HEAD_DIM:(h + 1) * HEAD_DIM, :].astype(BF16)
        qaug_ref[0, h, HEAD_DIM:KAUG, :] = ones_rows
    kt_ref[0] = zt[D_ATT:2 * D_ATT, :]
    vt = zt[2 * D_ATT:3 * D_ATT, :]
    vt_ref[0] = vt
    vtb_ref[0] = vt.astype(BF16)
    lft_ref[0] = _log_sigmoid(zt[3 * D_ATT:3 * D_ATT + N_HEADS, :] + bfc_ref[:, 0:1])


def _inproj_prompt(x, w1, wt, bf_row, bf_col, tri, place, tm):
    bsz, seq, _ = x.shape
    out_shape = (
        jax.ShapeDtypeStruct((bsz, seq, D_SSM), F32),
        jax.ShapeDtypeStruct((bsz, N_HEADS, seq, KAUG), BF16),
        jax.ShapeDtypeStruct((bsz, N_HEADS, KAUG, seq), BF16),
        jax.ShapeDtypeStruct((bsz, D_ATT, seq), F32),
        jax.ShapeDtypeStruct((bsz, D_ATT, seq), F32),
        jax.ShapeDtypeStruct((bsz, D_ATT, seq), BF16),
        jax.ShapeDtypeStruct((bsz, N_HEADS, seq), F32),
    )
    in_specs = [
        pl.BlockSpec((1, tm, D_MODEL), lambda b, l: (b, l, 0)),
        _const_spec(w1.shape), _const_spec(wt.shape), _const_spec(bf_row.shape),
        _const_spec(bf_col.shape), _const_spec(tri.shape), _const_spec(place.shape),
    ]
    out_specs = (
        pl.BlockSpec((1, tm, D_SSM), lambda b, l: (b, l, 0)),
        pl.BlockSpec((1, N_HEADS, tm, KAUG), lambda b, l: (b, 0, l, 0)),
        pl.BlockSpec((1, N_HEADS, KAUG, tm), lambda b, l: (b, 0, 0, l)),
        pl.BlockSpec((1, D_ATT, tm), lambda b, l: (b, 0, l)),
        pl.BlockSpec((1, D_ATT, tm), lambda b, l: (b, 0, l)),
        pl.BlockSpec((1, D_ATT, tm), lambda b, l: (b, 0, l)),
        pl.BlockSpec((1, N_HEADS, tm), lambda b, l: (b, 0, l)),
    )
    return pl.pallas_call(
        functools.partial(_inproj_prompt_kernel, tm=tm),
        grid=(bsz, seq // tm), in_specs=in_specs, out_specs=out_specs, out_shape=out_shape,
        scratch_shapes=[pltpu.VMEM((SUBLANES, LANES), F32)],
        compiler_params=_params(("arbitrary", "arbitrary")),
        name="inproj_prompt",
    )(x, w1, wt, bf_row, bf_col, tri, place)


def _ssm_out(h_re, h_im, u, cmat_ref, d_ref, wglu_ref, bglu_ref):
    hb = jnp.concatenate([h_re.astype(BF16), h_im.astype(BF16)], axis=1)
    y = _dot(hb, cmat_ref[...]) + d_ref[...] * u
    y = jax.nn.gelu(y)
    return y * jax.nn.sigmoid(_dot(y.astype(BF16), wglu_ref[...]) + bglu_ref[...])


def _ssm_prompt_kernel(u_ref, bmat_ref, cmat_ref, tab_ref, d_ref, wglu_ref, bglu_ref,
                       y_ref, hre_ref, him_ref, bre, bim, cre, cim, *, ts, lc):
    @pl.when(pl.program_id(1) == 0)
    def _():
        cre[...] = jnp.zeros_like(cre)
        cim[...] = jnp.zeros_like(cim)

    u = u_ref[0]
    bu = _dot(u.astype(BF16), bmat_ref[...])
    bre[...] = bu[:, :N_STATE]
    bim[...] = bu[:, N_STATE:]

    for c in range(N_STATE // lc):
        cols = slice(c * lc, (c + 1) * lc)

        def slab(i, carry, cols=cols):
            pr, pi = carry
            r0 = pl.multiple_of(i * SUBLANES, SUBLANES)
            xr = bre[pl.ds(r0, SUBLANES), cols]
            xi = bim[pl.ds(r0, SUBLANES), cols]
            for s, k in enumerate((1, 2, 4)):
                mr = tab_ref[s, :, cols]
                mi = tab_ref[3 + s, :, cols]
                sr = pltpu.roll(xr, k, 0)
                si = pltpu.roll(xi, k, 0)
                xr, xi = xr + (mr * sr - mi * si), xi + (mr * si + mi * sr)
            qr = tab_ref[6, :, cols]
            qi = tab_ref[7, :, cols]
            xr, xi = xr + (qr * pr - qi * pi), xi + (qr * pi + qi * pr)
            bre[pl.ds(r0, SUBLANES), cols] = xr
            bim[pl.ds(r0, SUBLANES), cols] = xi
            return (jnp.broadcast_to(xr[SUBLANES - 1:SUBLANES, :], xr.shape),
                    jnp.broadcast_to(xi[SUBLANES - 1:SUBLANES, :], xi.shape))

        lr, li = lax.fori_loop(0, ts // SUBLANES, slab, (cre[:, cols], cim[:, cols]))
        cre[:, cols] = lr
        cim[:, cols] = li

    y_ref[0] = _ssm_out(bre[...], bim[...], u, cmat_ref, d_ref, wglu_ref, bglu_ref).astype(y_ref.dtype)
    hre_ref[0] = cre[...]
    him_ref[0] = cim[...]


def _ssm_prompt(u, bmat, cmat, tab, d_row, wglu, bglu, ts, lc=512):
    bsz, seq, _ = u.shape
    out_shape = (
        jax.ShapeDtypeStruct((bsz, seq, D_SSM), BF16),
        jax.ShapeDtypeStruct((bsz, SUBLANES, N_STATE), F32),
        jax.ShapeDtypeStruct((bsz, SUBLANES, N_STATE), F32),
    )
    in_specs = [
        pl.BlockSpec((1, ts, D_SSM), lambda b, l: (b, l, 0)),
        _const_spec(bmat.shape), _const_spec(cmat.shape), _const_spec(tab.shape),
        _const_spec(d_row.shape), _const_spec(wglu.shape), _const_spec(bglu.shape),
    ]
    out_specs = (
        pl.BlockSpec((1, ts, D_SSM), lambda b, l: (b, l, 0)),
        pl.BlockSpec((1, SUBLANES, N_STATE), lambda b, l: (b, 0, 0)),
        pl.BlockSpec((1, SUBLANES, N_STATE), lambda b, l: (b, 0, 0)),
    )
    return pl.pallas_call(
        functools.partial(_ssm_prompt_kernel, ts=ts, lc=lc),
        grid=(bsz, seq // ts), in_specs=in_specs, out_specs=out_specs, out_shape=out_shape,
        scratch_shapes=[pltpu.VMEM((ts, N_STATE), F32), pltpu.VMEM((ts, N_STATE), F32),
                        pltpu.VMEM((SUBLANES, N_STATE), F32), pltpu.VMEM((SUBLANES, N_STATE), F32)],
        compiler_params=_params(("arbitrary", "arbitrary")),
        name="ssm_prompt",
    )(u, bmat, cmat, tab, d_row, wglu, bglu)


def _ssm_sample_kernel(u_ref, h0r_ref, h0i_ref, ar_ref, ai_ref, bmat_ref, cmat_ref, d_ref, wglu_ref,
                       bglu_ref, y_ref, hr_ref, hi_ref):
    u = u_ref[...]
    bu = _dot(u.astype(BF16), bmat_ref[...])
    ar = ar_ref[...]
    ai = ai_ref[...]
    h0r = h0r_ref[...]
    h0i = h0i_ref[...]
    hr = bu[:, :N_STATE] + (ar * h0r - ai * h0i)
    hi = bu[:, N_STATE:] + (ar * h0i + ai * h0r)
    hr_ref[...] = hr
    hi_ref[...] = hi
    y_ref[...] = _ssm_out(hr, hi, u, cmat_ref, d_ref, wglu_ref, bglu_ref).astype(y_ref.dtype)


def _ssm_sample(u, h0r, h0i, a_re, a_im, bmat, cmat, d_row, wglu, bglu):
    n = u.shape[0]
    args = (u, h0r, h0i, a_re, a_im, bmat, cmat, d_row, wglu, bglu)
    return pl.pallas_call(
        _ssm_sample_kernel,
        grid=(1,), in_specs=[_const_spec(a.shape) for a in args],
        out_specs=(_const_spec((n, D_SSM)), _const_spec((n, N_STATE)), _const_spec((n, N_STATE))),
        out_shape=(jax.ShapeDtypeStruct((n, D_SSM), BF16), jax.ShapeDtypeStruct((n, N_STATE), F32),
                   jax.ShapeDtypeStruct((n, N_STATE), F32)),
        compiler_params=_params(("arbitrary",)),
        name="ssm_sample",
    )(*args)


def _attn_prompt_kernel(qaug_ref, kaug_ref, vt_ref, o_ref, *, tq, heads):
    qi = pl.program_id(2)
    outs = []
    for hh in range(heads):
        qa = qaug_ref[0, hh]

        def block(kb, carry, masked, hh=hh, qa=qa):
            m, l, acc = carry
            k0 = pl.multiple_of(kb * tq, tq)
            s = _dot(kaug_ref[0, hh, pl.ds(k0, tq), :], qa)
            if masked:
                krow = lax.broadcasted_iota(jnp.int32, (tq, tq), 0)
                qcol = lax.broadcasted_iota(jnp.int32, (tq, tq), 1)
                s = jnp.where(krow <= qcol, s, NEG_BIG)
            m_new = jnp.maximum(m, jnp.max(s, axis=0, keepdims=True))
            alpha = jnp.exp(m - m_new)
            p = jnp.exp(s - m_new)
            l = alpha * l + jnp.sum(p, axis=0, keepdims=True)
            vt = vt_ref[0, hh * HEAD_DIM:(hh + 1) * HEAD_DIM, pl.ds(k0, tq)]
            acc = alpha * acc + _dot(vt, p.astype(BF16))
            return m_new, l, acc

        init = (jnp.full((1, tq), NEG_BIG, F32), jnp.zeros((1, tq), F32),
                jnp.zeros((HEAD_DIM, tq), F32))
        carry = lax.fori_loop(0, qi, functools.partial(block, masked=False), init)
        m, l, acc = block(qi, carry, True)
        outs.append((acc / l).T)
    o_ref[0] = jnp.concatenate(outs, axis=1).astype(o_ref.dtype)


def _attn_prompt(qaug, kaug, vtb, tq, heads=2):
    bsz, _, _, seq = qaug.shape
    width = heads * HEAD_DIM
    in_specs = [
        pl.BlockSpec((1, heads, KAUG, tq), lambda b, h, q: (b, h, 0, q)),
        pl.BlockSpec((1, heads, seq, KAUG), lambda b, h, q: (b, h, 0, 0)),
        pl.BlockSpec((1, width, seq), lambda b, h, q: (b, h, 0)),
    ]
    return pl.pallas_call(
        functools.partial(_attn_prompt_kernel, tq=tq, heads=heads),
        grid=(bsz, N_HEADS // heads, seq // tq),
        in_specs=in_specs,
        out_specs=pl.BlockSpec((1, tq, width), lambda b, h, q: (b, q, h)),
        out_shape=jax.ShapeDtypeStruct((bsz, seq, D_ATT), BF16),
        compiler_params=_params(("arbitrary", "arbitrary", "arbitrary")),
        name="attn_prompt",
    )(qaug, kaug, vtb)


def _mix_kernel(x_ref, ys_ref, ya_ref, wgs_ref, wga_ref, wbs_ref, wba_ref, wo_ref, g_ref, b_ref, o_ref):
    x = x_ref[0]
    xb = x.astype(BF16)
    gs = jax.nn.sigmoid(_dot(xb, wgs_ref[...]))
    ga = jax.nn.sigmoid(_dot(xb, wga_ref[...]))
    mixed = gs * _dot(ys_ref[0], wbs_ref[...]) + ga * _dot(ya_ref[0], wba_ref[...])
    t = ALPHA * x + _dot(mixed.astype(BF16), wo_ref[...])
    o_ref[0] = _layer_norm(t, g_ref[...], b_ref[...])


def _mix(x, ys, ya, wgs, wga, wbs, wba, wo, g, b, tm):
    bsz, seq, _ = x.shape
    ws = (wgs, wga, wbs, wba, wo, g, b)
    in_specs = [
        pl.BlockSpec((1, tm, D_MODEL), lambda i, l: (i, l, 0)),
        pl.BlockSpec((1, tm, D_SSM), lambda i, l: (i, l, 0)),
        pl.BlockSpec((1, tm, D_ATT), lambda i, l: (i, l, 0)),
    ] + [_const_spec(w.shape) for w in ws]
    return pl.pallas_call(
        _mix_kernel, grid=(bsz, seq // tm), in_specs=in_specs,
        out_specs=pl.BlockSpec((1, tm, D_MODEL), lambda i, l: (i, l, 0)),
        out_shape=jax.ShapeDtypeStruct((bsz, seq, D_MODEL), F32),
        compiler_params=_params(("arbitrary", "arbitrary")),
        name="mix_ln1",
    )(x, ys, ya, *ws)


def _ffn_tail(x, a, am1, am2, bgate, cw_ref, cb_ref, wdn_ref, g_ref, b_ref, wpg_ref, bpg_ref, wple_ref, p):
    c = cb_ref[...] + cw_ref[0:1, :] * am2 + cw_ref[1:2, :] * am1 + cw_ref[2:3, :] * a
    hcat = (jax.nn.gelu(c) * bgate).astype(BF16)
    x2 = _layer_norm(ALPHA * x + _dot(hcat, wdn_ref[...]), g_ref[...], b_ref[...])
    gate = jax.nn.sigmoid(_dot(x2.astype(BF16), wpg_ref[...]) + bpg_ref[...])
    return x2 + gate * _dot(p.astype(BF16), wple_ref[...])


def _ffn_prompt_kernel(x_ref, p_ref, wup_ref, cw_ref, cb_ref, wdn_ref, g_ref, b_ref, wpg_ref, bpg_ref,
                       wple_ref, o_ref, tail_ref, prev_ref, *, tm):
    @pl.when(pl.program_id(1) == 0)
    def _():
        prev_ref[...] = jnp.zeros_like(prev_ref)

    x = x_ref[0]
    up = _dot(x.astype(BF16), wup_ref[...])
    a = up[:, :D_FF]
    row = lax.broadcasted_iota(jnp.int32, (tm, D_FF), 0)
    p1 = jnp.broadcast_to(prev_ref[SUBLANES - 1:SUBLANES, :], (tm, D_FF))
    p2 = jnp.broadcast_to(prev_ref[SUBLANES - 2:SUBLANES - 1, :], (tm, D_FF))
    am1 = jnp.where(row == 0, p1, pltpu.roll(a, 1, 0))
    am2 = jnp.where(row == 0, p2, jnp.where(row == 1, p1, pltpu.roll(a, 2, 0)))
    tail = a[tm - SUBLANES:, :]
    prev_ref[...] = tail
    tail_ref[0] = tail
    o_ref[0] = _ffn_tail(x, a, am1, am2, up[:, D_FF:], cw_ref, cb_ref, wdn_ref, g_ref, b_ref,
                         wpg_ref, bpg_ref, wple_ref, p_ref[0])


def _ffn_sample_kernel(x_ref, p_ref, s0_ref, s1_ref, wup_ref, cw_ref, cb_ref, wdn_ref, g_ref, b_ref,
                       wpg_ref, bpg_ref, wple_ref, o_ref, a_ref):
    x = x_ref[...]
    up = _dot(x.astype(BF16), wup_ref[...])
    a = up[:, :D_FF]
    a_ref[...] = a
    o_ref[...] = _ffn_tail(x, a, s1_ref[...], s0_ref[...], up[:, D_FF:], cw_ref, cb_ref, wdn_ref, g_ref,
                           b_ref, wpg_ref, bpg_ref, wple_ref, p_ref[...])


def _ffn_prompt(x, p, ws, tm):
    bsz, seq, _ = x.shape
    in_specs = [
        pl.BlockSpec((1, tm, D_MODEL), lambda i, l: (i, l, 0)),
        pl.BlockSpec((1, tm, p.shape[-1]), lambda i, l: (i, l, 0)),
    ] + [_const_spec(w.shape) for w in ws]
    return pl.pallas_call(
        functools.partial(_ffn_prompt_kernel, tm=tm), grid=(bsz, seq // tm), in_specs=in_specs,
        out_specs=(pl.BlockSpec((1, tm, D_MODEL), lambda i, l: (i, l, 0)),
                   pl.BlockSpec((1, SUBLANES, D_FF), lambda i, l: (i, 0, 0))),
        out_shape=(jax.ShapeDtypeStruct((bsz, seq, D_MODEL), F32),
                   jax.ShapeDtypeStruct((bsz, SUBLANES, D_FF), F32)),
        scratch_shapes=[pltpu.VMEM((SUBLANES, D_FF), F32)],
        compiler_params=_params(("arbitrary", "arbitrary")),
        name="ffn_prompt",
    )(x, p, *ws)


def _ffn_sample(x, p, s0, s1, ws):
    n = x.shape[0]
    args = (x, p, s0, s1) + tuple(ws)
    return pl.pallas_call(
        _ffn_sample_kernel, grid=(1,), in_specs=[_const_spec(a.shape) for a in args],
        out_specs=(_const_spec((n, D_MODEL)), _const_spec((n, D_FF))),
        out_shape=(jax.ShapeDtypeStruct((n, D_MODEL), F32), jax.ShapeDtypeStruct((n, D_FF), F32)),
        compiler_params=_params(("arbitrary",)),
        name="ffn_sample",
    )(*args)


def _inproj_sample_kernel(x_ref, wu_ref, wt_ref, bfc_ref, u_ref, qt_ref, kt_ref, vt_ref, lft_ref):
    xb = x_ref[...].astype(BF16)
    u_ref[...] = _dot(xb, wu_ref[...])
    zt = _dot_nt(wt_ref[...], xb)
    qt_ref[...] = zt[0:D_ATT, :]
    kt_ref[...] = zt[D_ATT:2 * D_ATT, :]
    vt_ref[...] = zt[2 * D_ATT:3 * D_ATT, :]
    lft_ref[...] = _log_sigmoid(zt[3 * D_ATT:3 * D_ATT + N_HEADS, :] + bfc_ref[:, 0:1])


def _inproj_sample(x, wu, wt, bf_col):
    n = x.shape[0]
    args = (x, wu, wt, bf_col)
    shp = lambda r: jax.ShapeDtypeStruct((r, n), F32)
    return pl.pallas_call(
        _inproj_sample_kernel, grid=(1,), in_specs=[_const_spec(a.shape) for a in args],
        out_specs=(_const_spec((n, D_SSM)), _const_spec((D_ATT, n)), _const_spec((D_ATT, n)),
                   _const_spec((D_ATT, n)), _const_spec((N_HEADS, n))),
        out_shape=(jax.ShapeDtypeStruct((n, D_SSM), F32), shp(D_ATT), shp(D_ATT), shp(D_ATT), shp(N_HEADS)),
        compiler_params=_params(("arbitrary",)),
        name="inproj_sample",
    )(*args)


def _decode_kernel(pt_ref, *refs, pg, nj):
    del pt_ref
    k_refs = refs[0:pg]
    v_refs = refs[pg:2 * pg]
    lf_refs = refs[2 * pg:3 * pg]
    qt_ref, ktn_ref, vtn_ref, lfn_ref, o_ref, qb_ref, m_ref, l_ref, acc_ref, carry_ref = refs[3 * pg:]
    b = pl.program_id(0)
    j = pl.program_id(1)
    nb = qt_ref.shape[-1]

    def column(x):
        lane = lax.broadcasted_iota(jnp.int32, x.shape, 1)
        col = jnp.sum(jnp.where(lane == b, x, 0.0), axis=1, keepdims=True)
        return jnp.broadcast_to(col, (x.shape[0], LANES))

    @pl.when(jnp.logical_and(b == 0, j == 0))
    def _():
        o_ref[...] = jnp.zeros_like(o_ref)

    @pl.when(j == 0)
    def _():
        for h in range(N_HEADS):
            qb_ref[h] = column(qt_ref[h])
        m_ref[...] = jnp.full_like(m_ref, NEG_BIG)
        l_ref[...] = jnp.zeros_like(l_ref)
        acc_ref[...] = jnp.zeros_like(acc_ref)
        carry_ref[...] = column(lfn_ref[...])

    lane = lax.broadcasted_iota(jnp.int32, (N_HEADS, LANES), 1)
    sub = lax.broadcasted_iota(jnp.int32, (N_HEADS, LANES), 0)
    for i in reversed(range(pg)):
        lf = lf_refs[i][0]
        sx = lf
        k = 1
        while k < LANES:
            sx = sx + jnp.where(lane < LANES - k, pltpu.roll(sx, LANES - k, 1), 0.0)
            k *= 2
        carry = carry_ref[...]
        bias = carry + (sx - lf)
        carry_ref[...] = carry + jnp.broadcast_to(sx[:, 0:1], sx.shape)

        s = bias
        for h in range(N_HEADS):
            r = jnp.sum(k_refs[i][0, h] * qb_ref[h], axis=0, keepdims=True)
            s = s + jnp.where(sub == h, jnp.broadcast_to(r, s.shape), 0.0)
        m_old = m_ref[...]
        m_new = jnp.maximum(m_old, s)
        alpha = jnp.exp(m_old - m_new)
        p = jnp.exp(s - m_new)
        l_ref[...] = alpha * l_ref[...] + p
        m_ref[...] = m_new
        for h in range(N_HEADS):
            ah = jnp.broadcast_to(alpha[h:h + 1, :], (HEAD_DIM, LANES))
            ph = jnp.broadcast_to(p[h:h + 1, :], (HEAD_DIM, LANES))
            acc_ref[h] = acc_ref[h] * ah + v_refs[i][0, h] * ph

    @pl.when(j == nj - 1)
    def _():
        m = m_ref[...]
        l = l_ref[...]
        s_new = jnp.zeros((N_HEADS, LANES), F32)
        for h in range(N_HEADS):
            r = jnp.sum(column(ktn_ref[h]) * qb_ref[h], axis=0, keepdims=True)
            s_new = s_new + jnp.where(sub == h, jnp.broadcast_to(r, s_new.shape), 0.0)
        m_tot = jnp.maximum(jnp.broadcast_to(jnp.max(m, axis=1, keepdims=True), m.shape), s_new)
        w = jnp.exp(m - m_tot)
        p_new = jnp.exp(s_new - m_tot)
        l_tot = jnp.broadcast_to(jnp.sum(l * w, axis=1, keepdims=True), l.shape) + p_new
        inv = 1.0 / l_tot
        olane = lax.broadcasted_iota(jnp.int32, (HEAD_DIM, nb), 1)
        for h in range(N_HEADS):
            wh = jnp.broadcast_to(w[h:h + 1, :], (HEAD_DIM, LANES))
            num = jnp.sum(acc_ref[h] * wh, axis=1, keepdims=True)
            num = jnp.broadcast_to(num, (HEAD_DIM, LANES)) + column(vtn_ref[h]) * jnp.broadcast_to(
                p_new[h:h + 1, :], (HEAD_DIM, LANES))
            val = num * jnp.broadcast_to(inv[h:h + 1, :], (HEAD_DIM, LANES))
            o_ref[h] = jnp.where(olane == b, val, o_ref[h])


def _decode_attn(page_flat, kc, vc, lfc, qt, ktn, vtn, lfn, pg):
    nb = qt.shape[-1]
    n_pages = page_flat.shape[0] // nb
    nj = n_pages // pg
    page = kc.shape[-1]
    assert page == LANES and nb == LANES

    def page_map(i):
        return lambda b, j, pt: (pt[b * n_pages + (nj - 1 - j) * pg + i], 0, 0, 0)

    def lf_map(i):
        return lambda b, j, pt: (pt[b * n_pages + (nj - 1 - j) * pg + i], 0, 0)

    in_specs = ([pl.BlockSpec((1, N_HEADS, HEAD_DIM, page), page_map(i)) for i in range(pg)]
                + [pl.BlockSpec((1, N_HEADS, HEAD_DIM, page), page_map(i)) for i in range(pg)]
                + [pl.BlockSpec((1, N_HEADS, page), lf_map(i)) for i in range(pg)]
                + [pl.BlockSpec((N_HEADS, HEAD_DIM, nb), lambda b, j, pt: (0, 0, 0))] * 3
                + [pl.BlockSpec((N_HEADS, nb), lambda b, j, pt: (0, 0))])
    grid_spec = pltpu.PrefetchScalarGridSpec(
        num_scalar_prefetch=1, grid=(nb, nj), in_specs=in_specs,
        out_specs=pl.BlockSpec((N_HEADS, HEAD_DIM, nb), lambda b, j, pt: (0, 0, 0)),
        scratch_shapes=[pltpu.VMEM((N_HEADS, HEAD_DIM, LANES), F32),
                        pltpu.VMEM((N_HEADS, LANES), F32),
                        pltpu.VMEM((N_HEADS, LANES), F32),
                        pltpu.VMEM((N_HEADS, HEAD_DIM, LANES), F32),
                        pltpu.VMEM((N_HEADS, LANES), F32)])
    return pl.pallas_call(
        functools.partial(_decode_kernel, pg=pg, nj=nj),
        grid_spec=grid_spec,
        out_shape=jax.ShapeDtypeStruct((N_HEADS, HEAD_DIM, nb), F32),
        compiler_params=_params(("arbitrary", "arbitrary")),
        name="decode_attn",
    )(page_flat, *([kc] * pg), *([vc] * pg), *([lfc] * pg), qt, ktn, vtn, lfn)


def _discretize(a_re, a_im, log_dt, b_re, b_im):
    dt = jnp.exp(log_dt)[:, None]
    mag = jnp.exp(dt * a_re)
    ang = dt * a_im
    e_re = mag * jnp.cos(ang)
    e_im = mag * jnp.sin(ang)
    nr = e_re - 1.0
    ni = e_im
    den = a_re * a_re + a_im * a_im
    z_re = ((nr * a_re + ni * a_im) / den)[..., None]
    z_im = ((ni * a_re - nr * a_im) / den)[..., None]
    return e_re, e_im, z_re * b_re - z_im * b_im, z_re * b_im + z_im * b_re


def _block_diag(blocks):
    g, r, c = blocks.shape
    eye = jnp.eye(g, dtype=blocks.dtype)
    return (eye[:, None, :, None] * blocks[:, :, None, :]).reshape(g * r, g * c)


def _cmul(ar, ai, br, bi):
    return ar * br - ai * bi, ar * bi + ai * br


def _scan_tables(e_re, e_im):
    a1r, a1i = e_re.reshape(1, N_STATE), e_im.reshape(1, N_STATE)
    a2r, a2i = _cmul(a1r, a1i, a1r, a1i)
    a4r, a4i = _cmul(a2r, a2i, a2r, a2i)
    rows = jnp.arange(SUBLANES)[:, None]
    tabs_r, tabs_i = [], []
    for k, (pr, pi) in zip((1, 2, 4), ((a1r, a1i), (a2r, a2i), (a4r, a4i))):
        tabs_r.append(jnp.where(rows >= k, pr, 0.0))
        tabs_i.append(jnp.where(rows >= k, pi, 0.0))
    pw_r, pw_i = [a1r], [a1i]
    for _ in range(SUBLANES - 1):
        nr, ni = _cmul(pw_r[-1], pw_i[-1], a1r, a1i)
        pw_r.append(nr)
        pw_i.append(ni)
    return jnp.stack(tabs_r + tabs_i + [jnp.concatenate(pw_r, 0), jnp.concatenate(pw_i, 0)])


def kernel(x_prompt, x_sample, p_prompt, p_sample, cache_k, cache_v, cache_logf, page_table, state_ssm_re, state_ssm_im, state_conv, w_in, b_f, ssm_a_re, ssm_a_im, ssm_log_dt, ssm_b_re, ssm_b_im, ssm_c_re, ssm_c_im, ssm_d, w_glu, b_glu, w_br_ssm, w_br_att, w_o, ln1_g, ln1_b, w_up, conv_w, conv_b, w_down, ln2_g, ln2_b, w_ple_gate, b_ple_gate, w_ple):
    assert w_in.shape[0] == DEPTH
    bsz, seq, _ = x_prompt.shape
    nb = x_sample.shape[0]
    tm = 512

    win = w_in[0]
    o = [0, D_SSM, D_SSM + D_ATT, D_SSM + 2 * D_ATT, D_SSM + 3 * D_ATT, D_SSM + 3 * D_ATT + N_HEADS]
    wu, wq, wk, wv, wf = (win[:, o[i]:o[i + 1]] for i in range(5))
    wgs = win[:, o[5]:o[5] + D_MODEL].astype(BF16)
    wga = win[:, o[5] + D_MODEL:].astype(BF16)
    scale = HEAD_DIM ** -0.5
    wk_pad = jnp.pad(wk.reshape(D_MODEL, N_HEADS, HEAD_DIM), ((0, 0), (0, 0), (0, KAUG - HEAD_DIM)))
    wf_pad = jnp.pad(wf, ((0, 0), (0, LANES - N_HEADS)))
    w1 = jnp.concatenate([wu, wk_pad.reshape(D_MODEL, N_HEADS * KAUG), wf_pad], axis=1).astype(BF16)
    wt = jnp.concatenate([wq * scale, wk, wv, wf], axis=1).T.astype(BF16)
    wu_b = wu.astype(BF16)
    bf_row = jnp.pad(b_f, ((0, 0), (0, LANES - N_HEADS)))
    bf_col = jnp.broadcast_to(b_f[0][:, None], (N_HEADS, LANES))
    tri = jnp.tri(tm, dtype=BF16)
    pidx = jnp.arange(3 * LANES)
    place = jnp.zeros((3 * LANES, N_HEADS * KAUG), BF16).at[
        pidx, (pidx % LANES) * KAUG + HEAD_DIM + pidx // LANES].set(
        jnp.where(pidx % LANES < N_HEADS, 1.0, 0.0).astype(BF16), mode="drop")

    e_re, e_im, bb_re, bb_im = _discretize(ssm_a_re[0], ssm_a_im[0], ssm_log_dt[0], ssm_b_re[0], ssm_b_im[0])
    bmat = jnp.concatenate([_block_diag(jnp.swapaxes(bb_re, 1, 2)),
                            _block_diag(jnp.swapaxes(bb_im, 1, 2))], axis=1).astype(BF16)
    cmat = jnp.concatenate([_block_diag(jnp.swapaxes(ssm_c_re[0], 1, 2)),
                            _block_diag(jnp.swapaxes(-ssm_c_im[0], 1, 2))], axis=0).astype(BF16)
    tab = _scan_tables(e_re, e_im)
    a_re_row, a_im_row = e_re.reshape(1, N_STATE), e_im.reshape(1, N_STATE)
    wglu = w_glu[0].astype(BF16)
    mix_ws = (wgs, wga, w_br_ssm[0].astype(BF16), w_br_att[0].astype(BF16), w_o[0].astype(BF16), ln1_g, ln1_b)
    ffn_ws = (w_up[0].astype(BF16), conv_w[0], conv_b, w_down[0].astype(BF16), ln2_g, ln2_b,
              w_ple_gate[0].astype(BF16), b_ple_gate, w_ple[0].astype(BF16))

    u, kaug, qaug, kt, vt, vtb, lft = _inproj_prompt(x_prompt, w1, wt, bf_row, bf_col, tri, place, tm)
    ys, hre, him = _ssm_prompt(u, bmat, cmat, tab, ssm_d, wglu, b_glu, ts=256)
    ya = _attn_prompt(qaug, kaug, vtb, tq=512)
    x1 = _mix(x_prompt, ys, ya, *mix_ws, tm=tm)
    y_prompt, tail = _ffn_prompt(x1, p_prompt[0], ffn_ws, tm=256)

    k_prompt = jnp.transpose(kt.reshape(bsz, N_HEADS, HEAD_DIM, seq), (0, 3, 1, 2))[None]
    v_prompt = jnp.transpose(vt.reshape(bsz, N_HEADS, HEAD_DIM, seq), (0, 3, 1, 2))[None]
    logf_prompt = jnp.transpose(lft, (0, 2, 1))[None]
    ssm_re_prompt = hre[:, 0].reshape(1, bsz, N_GROUPS, STATE_DIM)
    ssm_im_prompt = him[:, 0].reshape(1, bsz, N_GROUPS, STATE_DIM)
    conv_prompt = tail[:, SUBLANES - (CONV_W - 1):][None]

    xs = x_sample[:, 0]
    us, qts, kts, vts, lfts = _inproj_sample(xs, wu_b, wt, bf_col)
    ys_s, hr_s, hi_s = _ssm_sample(us, state_ssm_re[0].reshape(nb, N_STATE), state_ssm_im[0].reshape(nb, N_STATE),
                                   a_re_row, a_im_row, bmat, cmat, ssm_d, wglu, b_glu)
    kc = jnp.transpose(cache_k[0], (0, 2, 3, 1))
    vc = jnp.transpose(cache_v[0], (0, 2, 3, 1))
    lfc = jnp.transpose(cache_logf[0], (0, 2, 1))
    shp3 = (N_HEADS, HEAD_DIM, nb)
    yat = _decode_attn(page_table.reshape(-1), kc, vc, lfc, qts.reshape(shp3), kts.reshape(shp3),
                       vts.reshape(shp3), lfts, pg=16)
    ya_s = yat.reshape(D_ATT, nb).T.astype(BF16)
    x1_s = _mix(xs[None], ys_s[None], ya_s[None], *mix_ws, tm=nb)[0]
    y_s, a_s = _ffn_sample(x1_s, p_sample[0, :, 0], state_conv[0, :, 0], state_conv[0, :, 1], ffn_ws)

    y_sample = y_s[:, None, :]
    k_sample = kts.T.reshape(1, nb, 1, N_HEADS, HEAD_DIM)
    v_sample = vts.T.reshape(1, nb, 1, N_HEADS, HEAD_DIM)
    logf_sample = lfts.T.reshape(1, nb, 1, N_HEADS)
    ssm_re_sample = hr_s.reshape(1, nb, N_GROUPS, STATE_DIM)
    ssm_im_sample = hi_s.reshape(1, nb, N_GROUPS, STATE_DIM)
    conv_sample = jnp.stack([state_conv[0, :, 1], a_s], axis=1)[None]

    return (y_prompt, y_sample, k_prompt, v_prompt, logf_prompt, ssm_re_prompt, ssm_im_prompt, conv_prompt,
            k_sample, v_sample, logf_sample, ssm_re_sample, ssm_im_sample, conv_sample)
```

```python
import functools

import jax
import jax.numpy as jnp
from jax import lax
from jax.experimental import pallas as pl
from jax.experimental.pallas import tpu as pltpu

F32 = jnp.float32
BF16 = jnp.bfloat16

D_MODEL = 1024
D_SSM = 512
SSM_GROUP = 16
N_GROUPS = 32
STATE_DIM = 64
N_STATE = N_GROUPS * STATE_DIM
N_HEADS = 8
HEAD_DIM = 64
D_ATT = 512
D_FF = 2816
CONV_W = 3
DEPTH = 1
ALPHA = (2 * DEPTH) ** 0.25
LN_EPS = 1e-5
NEG_BIG = -1e30
LANES = 128
SUBLANES = 8
KAUG = 128
VAUG = 80
LOG2E = 1.4426950408889634
VMEM_LIMIT = 56 * 1024 * 1024


def _dot(a, b):
    return jnp.dot(a, b, preferred_element_type=F32)


def _dot_nt(a, b):
    return lax.dot_general(a, b, (((1,), (1,)), ((), ())), preferred_element_type=F32)


def _log_sigmoid(x):
    return jnp.minimum(x, 0.0) - jnp.log1p(jnp.exp(-jnp.abs(x)))


def _split3(x):
    hi = x.astype(BF16)
    r1 = x - hi.astype(F32)
    mid = r1.astype(BF16)
    lo = (r1 - mid.astype(F32)).astype(BF16)
    return hi, mid, lo


def _layer_norm(t, g, b):
    mu = jnp.mean(t, axis=-1, keepdims=True)
    c = t - mu
    var = jnp.mean(c * c, axis=-1, keepdims=True)
    return c * lax.rsqrt(var + LN_EPS) * g + b


def _params(sem):
    return pltpu.CompilerParams(dimension_semantics=sem, vmem_limit_bytes=VMEM_LIMIT)


def _const_spec(shape):
    nd = len(shape)
    return pl.BlockSpec(shape, lambda *_: (0,) * nd)


def _inproj_prompt_kernel(x_ref, w1_ref, wt_ref, bfr_ref, bfc_ref, tri_ref, place_ref,
                          u_ref, kaug_ref, qaug_ref, kt_ref, vt_ref, vtb_ref, lft_ref, carry_ref, *, tm):
    @pl.when(pl.program_id(1) == 0)
    def _():
        carry_ref[...] = jnp.zeros_like(carry_ref)

    xb = x_ref[0].astype(BF16)
    z1 = _dot(xb, w1_ref[...])
    u_ref[0] = z1[:, :D_SSM]
    kpad = z1[:, D_SSM:D_SSM + N_HEADS * KAUG]
    f_logit = z1[:, D_SSM + N_HEADS * KAUG:]

    lane = lax.broadcasted_iota(jnp.int32, (tm, LANES), 1)
    lf = jnp.where(lane < N_HEADS, _log_sigmoid(f_logit + bfr_ref[...]), 0.0)
    hi, mid, lo = _split3(lf)
    cum = _dot(tri_ref[...], jnp.concatenate([hi, mid, lo], axis=1))
    fcum = (cum[:, :LANES] + cum[:, LANES:2 * LANES]) + cum[:, 2 * LANES:] + carry_ref[0:1, :]
    carry_ref[...] = jnp.broadcast_to(fcum[tm - 1:tm, :], carry_ref.shape)
    nhi, nmid, nlo = _split3(fcum * (-LOG2E))
    kaug = kpad + _dot(jnp.concatenate([nhi, nmid, nlo], axis=1), place_ref[...])
    kaug = kaug.astype(BF16)
    for h in range(N_HEADS):
        kaug_ref[0, h] = kaug[:, h * KAUG:(h + 1) * KAUG]

    zt = _dot_nt(wt_ref[...], xb)
    row = lax.broadcasted_iota(jnp.int32, (KAUG - HEAD_DIM, tm), 0)
    ones_rows = jnp.where(row < 3, 1.0, 0.0).astype(BF16)
    for h in range(N_HEADS):
        qaug_ref[0, h, 0:HEAD_DIM, :] = (zt[h * HEAD_DIM:(h + 1) * HEAD_DIM, :] * LOG2E).astype(BF16)
        qaug_ref[0, h, HEAD_DIM:KAUG, :] = ones_rows
    kt_ref[0] = zt[D_ATT:2 * D_ATT, :]
    vt = zt[2 * D_ATT:3 * D_ATT, :]
    vt_ref[0] = vt
    vrow = lax.broadcasted_iota(jnp.int32, (VAUG - HEAD_DIM, tm), 0)
    one_row = jnp.where(vrow == 0, 1.0, 0.0).astype(BF16)
    for h in range(N_HEADS):
        vtb_ref[0, h, 0:HEAD_DIM, :] = vt[h * HEAD_DIM:(h + 1) * HEAD_DIM, :].astype(BF16)
        vtb_ref[0, h, HEAD_DIM:VAUG, :] = one_row
    lft_ref[0] = _log_sigmoid(zt[3 * D_ATT:3 * D_ATT + N_HEADS, :] + bfc_ref[:, 0:1])


def _inproj_prompt(x, w1, wt, bf_row, bf_col, tri, place, tm):
    bsz, seq, _ = x.shape
    out_shape = (
        jax.ShapeDtypeStruct((bsz, seq, D_SSM), F32),
        jax.ShapeDtypeStruct((bsz, N_HEADS, seq, KAUG), BF16),
        jax.ShapeDtypeStruct((bsz, N_HEADS, KAUG, seq), BF16),
        jax.ShapeDtypeStruct((bsz, D_ATT, seq), F32),
        jax.ShapeDtypeStruct((bsz, D_ATT, seq), F32),
        jax.ShapeDtypeStruct((bsz, N_HEADS, VAUG, seq), BF16),
        jax.ShapeDtypeStruct((bsz, N_HEADS, seq), F32),
    )
    in_specs = [
        pl.BlockSpec((1, tm, D_MODEL), lambda b, l: (b, l, 0)),
        _const_spec(w1.shape), _const_spec(wt.shape), _const_spec(bf_row.shape),
        _const_spec(bf_col.shape), _const_spec(tri.shape), _const_spec(place.shape),
    ]
    out_specs = (
        pl.BlockSpec((1, tm, D_SSM), lambda b, l: (b, l, 0)),
        pl.BlockSpec((1, N_HEADS, tm, KAUG), lambda b, l: (b, 0, l, 0)),
        pl.BlockSpec((1, N_HEADS, KAUG, tm), lambda b, l: (b, 0, 0, l)),
        pl.BlockSpec((1, D_ATT, tm), lambda b, l: (b, 0, l)),
        pl.BlockSpec((1, D_ATT, tm), lambda b, l: (b, 0, l)),
        pl.BlockSpec((1, N_HEADS, VAUG, tm), lambda b, l: (b, 0, 0, l)),
        pl.BlockSpec((1, N_HEADS, tm), lambda b, l: (b, 0, l)),
    )
    return pl.pallas_call(
        functools.partial(_inproj_prompt_kernel, tm=tm),
        grid=(bsz, seq // tm), in_specs=in_specs, out_specs=out_specs, out_shape=out_shape,
        scratch_shapes=[pltpu.VMEM((SUBLANES, LANES), F32)],
        compiler_params=_params(("arbitrary", "arbitrary")),
        name="inproj_prompt",
    )(x, w1, wt, bf_row, bf_col, tri, place)


def _ssm_out(h_re, h_im, u, cmat_ref, d_ref, wglu_ref, bglu_ref):
    hb = jnp.concatenate([h_re.astype(BF16), h_im.astype(BF16)], axis=1)
    y = _dot(hb, cmat_ref[...]) + d_ref[...] * u
    y = jax.nn.gelu(y)
    return y * jax.nn.sigmoid(_dot(y.astype(BF16), wglu_ref[...]) + bglu_ref[...])


def _ssm_prompt_kernel(u_ref, bmat_ref, cmat_ref, tab_ref, d_ref, wglu_ref, bglu_ref,
                       y_ref, hre_ref, him_ref, bre, bim, cre, cim, *, ts, lc):
    @pl.when(pl.program_id(1) == 0)
    def _():
        cre[...] = jnp.zeros_like(cre)
        cim[...] = jnp.zeros_like(cim)

    u = u_ref[0]
    bu = _dot(u.astype(BF16), bmat_ref[...])
    bre[...] = bu[:, :N_STATE]
    bim[...] = bu[:, N_STATE:]

    for c in range(N_STATE // lc):
        cols = slice(c * lc, (c + 1) * lc)

        def slab(i, carry, cols=cols):
            pr, pi = carry
            r0 = pl.multiple_of(i * SUBLANES, SUBLANES)
            xr = bre[pl.ds(r0, SUBLANES), cols]
            xi = bim[pl.ds(r0, SUBLANES), cols]
            for s, k in enumerate((1, 2, 4)):
                mr = tab_ref[s, :, cols]
                mi = tab_ref[3 + s, :, cols]
                sr = pltpu.roll(xr, k, 0)
                si = pltpu.roll(xi, k, 0)
                xr, xi = xr + (mr * sr - mi * si), xi + (mr * si + mi * sr)
            qr = tab_ref[6, :, cols]
            qi = tab_ref[7, :, cols]
            xr, xi = xr + (qr * pr - qi * pi), xi + (qr * pi + qi * pr)
            bre[pl.ds(r0, SUBLANES), cols] = xr
            bim[pl.ds(r0, SUBLANES), cols] = xi
            return (jnp.broadcast_to(xr[SUBLANES - 1:SUBLANES, :], xr.shape),
                    jnp.broadcast_to(xi[SUBLANES - 1:SUBLANES, :], xi.shape))

        lr, li = lax.fori_loop(0, ts // SUBLANES, slab, (cre[:, cols], cim[:, cols]))
        cre[:, cols] = lr
        cim[:, cols] = li

    y_ref[0] = _ssm_out(bre[...], bim[...], u, cmat_ref, d_ref, wglu_ref, bglu_ref).astype(y_ref.dtype)
    hre_ref[0] = cre[...]
    him_ref[0] = cim[...]


def _ssm_prompt(u, bmat, cmat, tab, d_row, wglu, bglu, ts, lc=512):
    bsz, seq, _ = u.shape
    out_shape = (
        jax.ShapeDtypeStruct((bsz, seq, D_SSM), BF16),
        jax.ShapeDtypeStruct((bsz, SUBLANES, N_STATE), F32),
        jax.ShapeDtypeStruct((bsz, SUBLANES, N_STATE), F32),
    )
    in_specs = [
        pl.BlockSpec((1, ts, D_SSM), lambda b, l: (b, l, 0)),
        _const_spec(bmat.shape), _const_spec(cmat.shape), _const_spec(tab.shape),
        _const_spec(d_row.shape), _const_spec(wglu.shape), _const_spec(bglu.shape),
    ]
    out_specs = (
        pl.BlockSpec((1, ts, D_SSM), lambda b, l: (b, l, 0)),
        pl.BlockSpec((1, SUBLANES, N_STATE), lambda b, l: (b, 0, 0)),
        pl.BlockSpec((1, SUBLANES, N_STATE), lambda b, l: (b, 0, 0)),
    )
    return pl.pallas_call(
        functools.partial(_ssm_prompt_kernel, ts=ts, lc=lc),
        grid=(bsz, seq // ts), in_specs=in_specs, out_specs=out_specs, out_shape=out_shape,
        scratch_shapes=[pltpu.VMEM((ts, N_STATE), F32), pltpu.VMEM((ts, N_STATE), F32),
                        pltpu.VMEM((SUBLANES, N_STATE), F32), pltpu.VMEM((SUBLANES, N_STATE), F32)],
        compiler_params=_params(("arbitrary", "arbitrary")),
        name="ssm_prompt",
    )(u, bmat, cmat, tab, d_row, wglu, bglu)


def _ssm_sample_kernel(u_ref, h0r_ref, h0i_ref, ar_ref, ai_ref, bmat_ref, cmat_ref, d_ref, wglu_ref,
                       bglu_ref, y_ref, hr_ref, hi_ref):
    u = u_ref[...]
    bu = _dot(u.astype(BF16), bmat_ref[...])
    ar = ar_ref[...]
    ai = ai_ref[...]
    h0r = h0r_ref[...]
    h0i = h0i_ref[...]
    hr = bu[:, :N_STATE] + (ar * h0r - ai * h0i)
    hi = bu[:, N_STATE:] + (ar * h0i + ai * h0r)
    hr_ref[...] = hr
    hi_ref[...] = hi
    y_ref[...] = _ssm_out(hr, hi, u, cmat_ref, d_ref, wglu_ref, bglu_ref).astype(y_ref.dtype)


def _ssm_sample(u, h0r, h0i, a_re, a_im, bmat, cmat, d_row, wglu, bglu):
    n = u.shape[0]
    args = (u, h0r, h0i, a_re, a_im, bmat, cmat, d_row, wglu, bglu)
    return pl.pallas_call(
        _ssm_sample_kernel,
        grid=(1,), in_specs=[_const_spec(a.shape) for a in args],
        out_specs=(_const_spec((n, D_SSM)), _const_spec((n, N_STATE)), _const_spec((n, N_STATE))),
        out_shape=(jax.ShapeDtypeStruct((n, D_SSM), BF16), jax.ShapeDtypeStruct((n, N_STATE), F32),
                   jax.ShapeDtypeStruct((n, N_STATE), F32)),
        compiler_params=_params(("arbitrary",)),
        name="ssm_sample",
    )(*args)


def _attn_prompt_kernel(qaug_ref, kaug_ref, vt_ref, o_ref, s_ref, *, tq, heads):
    qi = pl.program_id(2)

    def qk(kb, slot, hh):
        k0 = pl.multiple_of(kb * tq, tq)
        s_ref[slot, hh] = _dot(kaug_ref[0, hh, pl.ds(k0, tq), :], qaug_ref[0, hh])

    def softmax_pv(kb, slot, hh, state, masked):
        m, acc = state
        k0 = pl.multiple_of(kb * tq, tq)
        s = s_ref[slot, hh]
        if masked:
            krow = lax.broadcasted_iota(jnp.int32, (tq, tq), 0)
            qcol = lax.broadcasted_iota(jnp.int32, (tq, tq), 1)
            s = jnp.where(krow <= qcol, s, NEG_BIG)
        m_new = jnp.maximum(m, jnp.max(s, axis=0, keepdims=True))
        p = jnp.exp2(s - m_new).astype(BF16)
        pv = _dot(vt_ref[0, hh, :, pl.ds(k0, tq)], p)
        return m_new, jnp.exp2(m - m_new) * acc + pv

    def step(kb, cur, states):
        out = []
        qk(kb + 1, 1 - cur, 0)
        for hh in range(heads):
            if hh + 1 < heads:
                qk(kb + 1, 1 - cur, hh + 1)
            out.append(softmax_pv(kb, cur, hh, states[hh], False))
        return tuple(out)

    def finish(cur, states):
        outs = []
        for hh in range(heads):
            _, acc = softmax_pv(qi, cur, hh, states[hh], True)
            outs.append((acc[0:HEAD_DIM] / acc[HEAD_DIM:HEAD_DIM + 1]).T)
        o_ref[0] = jnp.concatenate(outs, axis=1).astype(o_ref.dtype)

    for hh in range(heads):
        qk(0, 0, hh)
    init = tuple((jnp.full((1, tq), NEG_BIG, F32), jnp.zeros((VAUG, tq), F32)) for _ in range(heads))
    states = lax.fori_loop(0, qi // 2, lambda i, st: step(2 * i + 1, 1, step(2 * i, 0, st)), init)
    odd = lax.rem(qi, 2) == 1

    @pl.when(odd)
    def _():
        finish(1, step(qi - 1, 0, states))

    @pl.when(jnp.logical_not(odd))
    def _():
        finish(0, states)


def _attn_prompt(qaug, kaug, vtb, tq, heads):
    bsz, _, _, seq = qaug.shape
    width = heads * HEAD_DIM
    in_specs = [
        pl.BlockSpec((1, heads, KAUG, tq), lambda b, h, q: (b, h, 0, q)),
        pl.BlockSpec((1, heads, seq, KAUG), lambda b, h, q: (b, h, 0, 0)),
        pl.BlockSpec((1, heads, VAUG, seq), lambda b, h, q: (b, h, 0, 0)),
    ]
    return pl.pallas_call(
        functools.partial(_attn_prompt_kernel, tq=tq, heads=heads),
        grid=(bsz, N_HEADS // heads, seq // tq),
        in_specs=in_specs,
        out_specs=pl.BlockSpec((1, tq, width), lambda b, h, q: (b, q, h)),
        out_shape=jax.ShapeDtypeStruct((bsz, seq, D_ATT), BF16),
        scratch_shapes=[pltpu.VMEM((2, heads, tq, tq), F32)],
        compiler_params=_params(("arbitrary", "arbitrary", "arbitrary")),
        name="attn_prompt",
    )(qaug, kaug, vtb)


def _mix_kernel(x_ref, ys_ref, ya_ref, wgs_ref, wga_ref, wbs_ref, wba_ref, wo_ref, g_ref, b_ref, o_ref):
    x = x_ref[0]
    xb = x.astype(BF16)
    gs = jax.nn.sigmoid(_dot(xb, wgs_ref[...]))
    ga = jax.nn.sigmoid(_dot(xb, wga_ref[...]))
    mixed = gs * _dot(ys_ref[0], wbs_ref[...]) + ga * _dot(ya_ref[0], wba_ref[...])
    t = ALPHA * x + _dot(mixed.astype(BF16), wo_ref[...])
    o_ref[0] = _layer_norm(t, g_ref[...], b_ref[...])


def _mix(x, ys, ya, wgs, wga, wbs, wba, wo, g, b, tm):
    bsz, seq, _ = x.shape
    ws = (wgs, wga, wbs, wba, wo, g, b)
    in_specs = [
        pl.BlockSpec((1, tm, D_MODEL), lambda i, l: (i, l, 0)),
        pl.BlockSpec((1, tm, D_SSM), lambda i, l: (i, l, 0)),
        pl.BlockSpec((1, tm, D_ATT), lambda i, l: (i, l, 0)),
    ] + [_const_spec(w.shape) for w in ws]
    return pl.pallas_call(
        _mix_kernel, grid=(bsz, seq // tm), in_specs=in_specs,
        out_specs=pl.BlockSpec((1, tm, D_MODEL), lambda i, l: (i, l, 0)),
        out_shape=jax.ShapeDtypeStruct((bsz, seq, D_MODEL), F32),
        compiler_params=_params(("arbitrary", "arbitrary")),
        name="mix_ln1",
    )(x, ys, ya, *ws)


def _ffn_tail(x, a, am1, am2, bgate, cw_ref, cb_ref, wdn_ref, g_ref, b_ref, wpg_ref, bpg_ref, wple_ref, p):
    c = cb_ref[...] + cw_ref[0:1, :] * am2 + cw_ref[1:2, :] * am1 + cw_ref[2:3, :] * a
    hcat = (jax.nn.gelu(c) * bgate).astype(BF16)
    x2 = _layer_norm(ALPHA * x + _dot(hcat, wdn_ref[...]), g_ref[...], b_ref[...])
    gate = jax.nn.sigmoid(_dot(x2.astype(BF16), wpg_ref[...]) + bpg_ref[...])
    return x2 + gate * _dot(p.astype(BF16), wple_ref[...])


def _ffn_prompt_kernel(x_ref, p_ref, wup_ref, cw_ref, cb_ref, wdn_ref, g_ref, b_ref, wpg_ref, bpg_ref,
                       wple_ref, o_ref, tail_ref, prev_ref, *, tm):
    @pl.when(pl.program_id(1) == 0)
    def _():
        prev_ref[...] = jnp.zeros_like(prev_ref)

    x = x_ref[0]
    up = _dot(x.astype(BF16), wup_ref[...])
    a = up[:, :D_FF]
    row = lax.broadcasted_iota(jnp.int32, (tm, D_FF), 0)
    p1 = jnp.broadcast_to(prev_ref[SUBLANES - 1:SUBLANES, :], (tm, D_FF))
    p2 = jnp.broadcast_to(prev_ref[SUBLANES - 2:SUBLANES - 1, :], (tm, D_FF))
    am1 = jnp.where(row == 0, p1, pltpu.roll(a, 1, 0))
    am2 = jnp.where(row == 0, p2, jnp.where(row == 1, p1, pltpu.roll(a, 2, 0)))
    tail = a[tm - SUBLANES:, :]
    prev_ref[...] = tail
    tail_ref[0] = tail
    o_ref[0] = _ffn_tail(x, a, am1, am2, up[:, D_FF:], cw_ref, cb_ref, wdn_ref, g_ref, b_ref,
                         wpg_ref, bpg_ref, wple_ref, p_ref[0])


def _ffn_sample_kernel(x_ref, p_ref, s0_ref, s1_ref, wup_ref, cw_ref, cb_ref, wdn_ref, g_ref, b_ref,
                       wpg_ref, bpg_ref, wple_ref, o_ref, a_ref):
    x = x_ref[...]
    up = _dot(x.astype(BF16), wup_ref[...])
    a = up[:, :D_FF]
    a_ref[...] = a
    o_ref[...] = _ffn_tail(x, a, s1_ref[...], s0_ref[...], up[:, D_FF:], cw_ref, cb_ref, wdn_ref, g_ref,
                           b_ref, wpg_ref, bpg_ref, wple_ref, p_ref[...])


def _ffn_prompt(x, p, ws, tm):
    bsz, seq, _ = x.shape
    in_specs = [
        pl.BlockSpec((1, tm, D_MODEL), lambda i, l: (i, l, 0)),
        pl.BlockSpec((1, tm, p.shape[-1]), lambda i, l: (i, l, 0)),
    ] + [_const_spec(w.shape) for w in ws]
    return pl.pallas_call(
        functools.partial(_ffn_prompt_kernel, tm=tm), grid=(bsz, seq // tm), in_specs=in_specs,
        out_specs=(pl.BlockSpec((1, tm, D_MODEL), lambda i, l: (i, l, 0)),
                   pl.BlockSpec((1, SUBLANES, D_FF), lambda i, l: (i, 0, 0))),
        out_shape=(jax.ShapeDtypeStruct((bsz, seq, D_MODEL), F32),
                   jax.ShapeDtypeStruct((bsz, SUBLANES, D_FF), F32)),
        scratch_shapes=[pltpu.VMEM((SUBLANES, D_FF), F32)],
        compiler_params=_params(("arbitrary", "arbitrary")),
        name="ffn_prompt",
    )(x, p, *ws)


def _ffn_sample(x, p, s0, s1, ws):
    n = x.shape[0]
    args = (x, p, s0, s1) + tuple(ws)
    return pl.pallas_call(
        _ffn_sample_kernel, grid=(1,), in_specs=[_const_spec(a.shape) for a in args],
        out_specs=(_const_spec((n, D_MODEL)), _const_spec((n, D_FF))),
        out_shape=(jax.ShapeDtypeStruct((n, D_MODEL), F32), jax.ShapeDtypeStruct((n, D_FF), F32)),
        compiler_params=_params(("arbitrary",)),
        name="ffn_sample",
    )(*args)


def _inproj_sample_kernel(x_ref, wu_ref, wt_ref, bfc_ref, u_ref, qt_ref, kt_ref, vt_ref, lft_ref):
    xb = x_ref[...].astype(BF16)
    u_ref[...] = _dot(xb, wu_ref[...])
    zt = _dot_nt(wt_ref[...], xb)
    qt_ref[...] = zt[0:D_ATT, :]
    kt_ref[...] = zt[D_ATT:2 * D_ATT, :]
    vt_ref[...] = zt[2 * D_ATT:3 * D_ATT, :]
    lft_ref[...] = _log_sigmoid(zt[3 * D_ATT:3 * D_ATT + N_HEADS, :] + bfc_ref[:, 0:1])


def _inproj_sample(x, wu, wt, bf_col):
    n = x.shape[0]
    args = (x, wu, wt, bf_col)
    shp = lambda r: jax.ShapeDtypeStruct((r, n), F32)
    return pl.pallas_call(
        _inproj_sample_kernel, grid=(1,), in_specs=[_const_spec(a.shape) for a in args],
        out_specs=(_const_spec((n, D_SSM)), _const_spec((D_ATT, n)), _const_spec((D_ATT, n)),
                   _const_spec((D_ATT, n)), _const_spec((N_HEADS, n))),
        out_shape=(jax.ShapeDtypeStruct((n, D_SSM), F32), shp(D_ATT), shp(D_ATT), shp(D_ATT), shp(N_HEADS)),
        compiler_params=_params(("arbitrary",)),
        name="inproj_sample",
    )(*args)


def _decode_kernel(pt_ref, *refs, pg, nj):
    del pt_ref
    k_refs = refs[0:pg]
    v_refs = refs[pg:2 * pg]
    lf_refs = refs[2 * pg:3 * pg]
    qt_ref, ktn_ref, vtn_ref, lfn_ref, o_ref, qb_ref, m_ref, l_ref, acc_ref, carry_ref = refs[3 * pg:]
    b = pl.program_id(0)
    j = pl.program_id(1)
    nb = qt_ref.shape[-1]

    def column(x):
        lane = lax.broadcasted_iota(jnp.int32, x.shape, 1)
        col = jnp.sum(jnp.where(lane == b, x, 0.0), axis=1, keepdims=True)
        return jnp.broadcast_to(col, (x.shape[0], LANES))

    @pl.when(jnp.logical_and(b == 0, j == 0))
    def _():
        o_ref[...] = jnp.zeros_like(o_ref)

    @pl.when(j == 0)
    def _():
        for h in range(N_HEADS):
            qb_ref[h] = column(qt_ref[h])
        m_ref[...] = jnp.full_like(m_ref, NEG_BIG)
        l_ref[...] = jnp.zeros_like(l_ref)
        acc_ref[...] = jnp.zeros_like(acc_ref)
        carry_ref[...] = column(lfn_ref[...])

    lane = lax.broadcasted_iota(jnp.int32, (N_HEADS, LANES), 1)
    sub = lax.broadcasted_iota(jnp.int32, (N_HEADS, LANES), 0)
    carry = carry_ref[...]
    logits = [None] * pg
    for i in reversed(range(pg)):
        lf = lf_refs[i][0]
        sx = lf
        k = 1
        while k < LANES:
            sx = sx + jnp.where(lane < LANES - k, pltpu.roll(sx, LANES - k, 1), 0.0)
            k *= 2
        logits[i] = carry + (sx - lf)
        carry = carry + jnp.broadcast_to(sx[:, 0:1], sx.shape)
    carry_ref[...] = carry

    for i in range(pg):
        s = logits[i]
        for h in range(N_HEADS):
            r = jnp.sum(k_refs[i][0, h] * qb_ref[h], axis=0, keepdims=True)
            s = s + jnp.where(sub == h, jnp.broadcast_to(r, s.shape), 0.0)
        logits[i] = s
    m_old = m_ref[...]
    m_new = m_old
    for s in logits:
        m_new = jnp.maximum(m_new, s)
    alpha = jnp.exp(m_old - m_new)
    probs = [jnp.exp(s - m_new) for s in logits]
    l_ref[...] = alpha * l_ref[...] + functools.reduce(lambda x, y: x + y, probs)
    m_ref[...] = m_new
    for h in range(N_HEADS):
        acc = acc_ref[h] * jnp.broadcast_to(alpha[h:h + 1, :], (HEAD_DIM, LANES))
        for i in range(pg):
            acc = acc + v_refs[i][0, h] * jnp.broadcast_to(probs[i][h:h + 1, :], (HEAD_DIM, LANES))
        acc_ref[h] = acc

    @pl.when(j == nj - 1)
    def _():
        m = m_ref[...]
        l = l_ref[...]
        s_new = jnp.zeros((N_HEADS, LANES), F32)
        for h in range(N_HEADS):
            r = jnp.sum(column(ktn_ref[h]) * qb_ref[h], axis=0, keepdims=True)
            s_new = s_new + jnp.where(sub == h, jnp.broadcast_to(r, s_new.shape), 0.0)
        m_tot = jnp.maximum(jnp.broadcast_to(jnp.max(m, axis=1, keepdims=True), m.shape), s_new)
        w = jnp.exp(m - m_tot)
        p_new = jnp.exp(s_new - m_tot)
        l_tot = jnp.broadcast_to(jnp.sum(l * w, axis=1, keepdims=True), l.shape) + p_new
        inv = 1.0 / l_tot
        olane = lax.broadcasted_iota(jnp.int32, (HEAD_DIM, nb), 1)
        for h in range(N_HEADS):
            wh = jnp.broadcast_to(w[h:h + 1, :], (HEAD_DIM, LANES))
            num = jnp.sum(acc_ref[h] * wh, axis=1, keepdims=True)
            num = jnp.broadcast_to(num, (HEAD_DIM, LANES)) + column(vtn_ref[h]) * jnp.broadcast_to(
                p_new[h:h + 1, :], (HEAD_DIM, LANES))
            val = num * jnp.broadcast_to(inv[h:h + 1, :], (HEAD_DIM, LANES))
            o_ref[h] = jnp.where(olane == b, val, o_ref[h])


def _decode_attn(page_flat, kc, vc, lfc, qt, ktn, vtn, lfn, pg):
    nb = qt.shape[-1]
    n_pages = page_flat.shape[0] // nb
    nj = n_pages // pg
    page = kc.shape[-1]
    assert page == LANES and nb == LANES

    def page_map(i):
        return lambda b, j, pt: (pt[b * n_pages + (nj - 1 - j) * pg + i], 0, 0, 0)

    def lf_map(i):
        return lambda b, j, pt: (pt[b * n_pages + (nj - 1 - j) * pg + i], 0, 0)

    in_specs = ([pl.BlockSpec((1, N_HEADS, HEAD_DIM, page), page_map(i)) for i in range(pg)]
                + [pl.BlockSpec((1, N_HEADS, HEAD_DIM, page), page_map(i)) for i in range(pg)]
                + [pl.BlockSpec((1, N_HEADS, page), lf_map(i)) for i in range(pg)]
                + [pl.BlockSpec((N_HEADS, HEAD_DIM, nb), lambda b, j, pt: (0, 0, 0))] * 3
                + [pl.BlockSpec((N_HEADS, nb), lambda b, j, pt: (0, 0))])
    grid_spec = pltpu.PrefetchScalarGridSpec(
        num_scalar_prefetch=1, grid=(nb, nj), in_specs=in_specs,
        out_specs=pl.BlockSpec((N_HEADS, HEAD_DIM, nb), lambda b, j, pt: (0, 0, 0)),
        scratch_shapes=[pltpu.VMEM((N_HEADS, HEAD_DIM, LANES), F32),
                        pltpu.VMEM((N_HEADS, LANES), F32),
                        pltpu.VMEM((N_HEADS, LANES), F32),
                        pltpu.VMEM((N_HEADS, HEAD_DIM, LANES), F32),
                        pltpu.VMEM((N_HEADS, LANES), F32)])
    return pl.pallas_call(
        functools.partial(_decode_kernel, pg=pg, nj=nj),
        grid_spec=grid_spec,
        out_shape=jax.ShapeDtypeStruct((N_HEADS, HEAD_DIM, nb), F32),
        compiler_params=_params(("arbitrary", "arbitrary")),
        name="decode_attn",
    )(page_flat, *([kc] * pg), *([vc] * pg), *([lfc] * pg), qt, ktn, vtn, lfn)


def _discretize(a_re, a_im, log_dt, b_re, b_im):
    dt = jnp.exp(log_dt)[:, None]
    mag = jnp.exp(dt * a_re)
    ang = dt * a_im
    e_re = mag * jnp.cos(ang)
    e_im = mag * jnp.sin(ang)
    nr = e_re - 1.0
    ni = e_im
    den = a_re * a_re + a_im * a_im
    z_re = ((nr * a_re + ni * a_im) / den)[..., None]
    z_im = ((ni * a_re - nr * a_im) / den)[..., None]
    return e_re, e_im, z_re * b_re - z_im * b_im, z_re * b_im + z_im * b_re


def _block_diag(blocks):
    g, r, c = blocks.shape
    eye = jnp.eye(g, dtype=blocks.dtype)
    return (eye[:, None, :, None] * blocks[:, :, None, :]).reshape(g * r, g * c)


def _cmul(ar, ai, br, bi):
    return ar * br - ai * bi, ar * bi + ai * br


def _scan_tables(e_re, e_im):
    a1r, a1i = e_re.reshape(1, N_STATE), e_im.reshape(1, N_STATE)
    a2r, a2i = _cmul(a1r, a1i, a1r, a1i)
    a4r, a4i = _cmul(a2r, a2i, a2r, a2i)
    rows = jnp.arange(SUBLANES)[:, None]
    tabs_r, tabs_i = [], []
    for k, (pr, pi) in zip((1, 2, 4), ((a1r, a1i), (a2r, a2i), (a4r, a4i))):
        tabs_r.append(jnp.where(rows >= k, pr, 0.0))
        tabs_i.append(jnp.where(rows >= k, pi, 0.0))
    pw_r, pw_i = [a1r], [a1i]
    for _ in range(SUBLANES - 1):
        nr, ni = _cmul(pw_r[-1], pw_i[-1], a1r, a1i)
        pw_r.append(nr)
        pw_i.append(ni)
    return jnp.stack(tabs_r + tabs_i + [jnp.concatenate(pw_r, 0), jnp.concatenate(pw_i, 0)])


def kernel(x_prompt, x_sample, p_prompt, p_sample, cache_k, cache_v, cache_logf, page_table, state_ssm_re, state_ssm_im, state_conv, w_in, b_f, ssm_a_re, ssm_a_im, ssm_log_dt, ssm_b_re, ssm_b_im, ssm_c_re, ssm_c_im, ssm_d, w_glu, b_glu, w_br_ssm, w_br_att, w_o, ln1_g, ln1_b, w_up, conv_w, conv_b, w_down, ln2_g, ln2_b, w_ple_gate, b_ple_gate, w_ple):
    assert w_in.shape[0] == DEPTH
    bsz, seq, _ = x_prompt.shape
    nb = x_sample.shape[0]
    tm = 512

    win = w_in[0]
    o = [0, D_SSM, D_SSM + D_ATT, D_SSM + 2 * D_ATT, D_SSM + 3 * D_ATT, D_SSM + 3 * D_ATT + N_HEADS]
    wu, wq, wk, wv, wf = (win[:, o[i]:o[i + 1]] for i in range(5))
    wgs = win[:, o[5]:o[5] + D_MODEL].astype(BF16)
    wga = win[:, o[5] + D_MODEL:].astype(BF16)
    scale = HEAD_DIM ** -0.5
    wk_pad = jnp.pad(wk.reshape(D_MODEL, N_HEADS, HEAD_DIM), ((0, 0), (0, 0), (0, KAUG - HEAD_DIM)))
    wf_pad = jnp.pad(wf, ((0, 0), (0, LANES - N_HEADS)))
    w1 = jnp.concatenate([wu, wk_pad.reshape(D_MODEL, N_HEADS * KAUG), wf_pad], axis=1).astype(BF16)
    wt = jnp.concatenate([wq * scale, wk, wv, wf], axis=1).T.astype(BF16)
    wu_b = wu.astype(BF16)
    bf_row = jnp.pad(b_f, ((0, 0), (0, LANES - N_HEADS)))
    bf_col = jnp.broadcast_to(b_f[0][:, None], (N_HEADS, LANES))
    tri = jnp.tri(tm, dtype=BF16)
    pidx = jnp.arange(3 * LANES)
    place = jnp.zeros((3 * LANES, N_HEADS * KAUG), BF16).at[
        pidx, (pidx % LANES) * KAUG + HEAD_DIM + pidx // LANES].set(
        jnp.where(pidx % LANES < N_HEADS, 1.0, 0.0).astype(BF16), mode="drop")

    e_re, e_im, bb_re, bb_im = _discretize(ssm_a_re[0], ssm_a_im[0], ssm_log_dt[0], ssm_b_re[0], ssm_b_im[0])
    bmat = jnp.concatenate([_block_diag(jnp.swapaxes(bb_re, 1, 2)),
                            _block_diag(jnp.swapaxes(bb_im, 1, 2))], axis=1).astype(BF16)
    cmat = jnp.concatenate([_block_diag(jnp.swapaxes(ssm_c_re[0], 1, 2)),
                            _block_diag(jnp.swapaxes(-ssm_c_im[0], 1, 2))], axis=0).astype(BF16)
    tab = _scan_tables(e_re, e_im)
    a_re_row, a_im_row = e_re.reshape(1, N_STATE), e_im.reshape(1, N_STATE)
    wglu = w_glu[0].astype(BF16)
    mix_ws = (wgs, wga, w_br_ssm[0].astype(BF16), w_br_att[0].astype(BF16), w_o[0].astype(BF16), ln1_g, ln1_b)
    ffn_ws = (w_up[0].astype(BF16), conv_w[0], conv_b, w_down[0].astype(BF16), ln2_g, ln2_b,
              w_ple_gate[0].astype(BF16), b_ple_gate, w_ple[0].astype(BF16))

    u, kaug, qaug, kt, vt, vtb, lft = _inproj_prompt(x_prompt, w1, wt, bf_row, bf_col, tri, place, tm)
    ys, hre, him = _ssm_prompt(u, bmat, cmat, tab, ssm_d, wglu, b_glu, ts=256)
    ya = _attn_prompt(qaug, kaug, vtb, tq=512, heads=4)
    x1 = _mix(x_prompt, ys, ya, *mix_ws, tm=tm)
    y_prompt, tail = _ffn_prompt(x1, p_prompt[0], ffn_ws, tm=256)

    k_prompt = jnp.transpose(kt.reshape(bsz, N_HEADS, HEAD_DIM, seq), (0, 3, 1, 2))[None]
    v_prompt = jnp.transpose(vt.reshape(bsz, N_HEADS, HEAD_DIM, seq), (0, 3, 1, 2))[None]
    logf_prompt = jnp.transpose(lft, (0, 2, 1))[None]
    ssm_re_prompt = hre[:, 0].reshape(1, bsz, N_GROUPS, STATE_DIM)
    ssm_im_prompt = him[:, 0].reshape(1, bsz, N_GROUPS, STATE_DIM)
    conv_prompt = tail[:, SUBLANES - (CONV_W - 1):][None]

    xs = x_sample[:, 0]
    us, qts, kts, vts, lfts = _inproj_sample(xs, wu_b, wt, bf_col)
    ys_s, hr_s, hi_s = _ssm_sample(us, state_ssm_re[0].reshape(nb, N_STATE), state_ssm_im[0].reshape(nb, N_STATE),
                                   a_re_row, a_im_row, bmat, cmat, ssm_d, wglu, b_glu)
    kc = jnp.transpose(cache_k[0], (0, 2, 3, 1))
    vc = jnp.transpose(cache_v[0], (0, 2, 3, 1))
    lfc = jnp.transpose(cache_logf[0], (0, 2, 1))
    shp3 = (N_HEADS, HEAD_DIM, nb)
    yat = _decode_attn(page_table.reshape(-1), kc, vc, lfc, qts.reshape(shp3), kts.reshape(shp3),
                       vts.reshape(shp3), lfts, pg=16)
    ya_s = yat.reshape(D_ATT, nb).T.astype(BF16)
    x1_s = _mix(xs[None], ys_s[None], ya_s[None], *mix_ws, tm=nb)[0]
    y_s, a_s = _ffn_sample(x1_s, p_sample[0, :, 0], state_conv[0, :, 0], state_conv[0, :, 1], ffn_ws)

    y_sample = y_s[:, None, :]
    k_sample = kts.T.reshape(1, nb, 1, N_HEADS, HEAD_DIM)
    v_sample = vts.T.reshape(1, nb, 1, N_HEADS, HEAD_DIM)
    logf_sample = lfts.T.reshape(1, nb, 1, N_HEADS)
    ssm_re_sample = hr_s.reshape(1, nb, N_GROUPS, STATE_DIM)
    ssm_im_sample = hi_s.reshape(1, nb, N_GROUPS, STATE_DIM)
    conv_sample = jnp.stack([state_conv[0, :, 1], a_s], axis=1)[None]

    return (y_prompt, y_sample, k_prompt, v_prompt, logf_prompt, ssm_re_prompt, ssm_im_prompt, conv_prompt,
            k_sample, v_sample, logf_sample, ssm_re_sample, ssm_im_sample, conv_sample)
```

```python
import functools
from typing import NamedTuple

import jax
import jax.numpy as jnp
from jax import lax
from jax.experimental import pallas as pl
from jax.experimental.pallas import tpu as pltpu

F32 = jnp.float32
BF16 = jnp.bfloat16

D_MODEL = 1024
D_SSM = 512
SSM_GROUP = 16
N_GROUPS = 32
STATE_DIM = 64
N_STATE = N_GROUPS * STATE_DIM
N_HEADS = 8
HEAD_DIM = 64
D_ATT = 512
D_FF = 2816
CONV_W = 3
DEPTH = 1
ALPHA = (2 * DEPTH) ** 0.25
LN_EPS = 1e-5
NEG_BIG = -1e30
LANES = 128
SUBLANES = 8
KAUG = 128
VAUG = 80
LOG2E = 1.4426950408889634
VMEM_LIMIT = 56 * 1024 * 1024


def _dot(a, b):
    return jnp.dot(a, b, preferred_element_type=F32)


def _dot_nt(a, b):
    return lax.dot_general(a, b, (((1,), (1,)), ((), ())), preferred_element_type=F32)


def _log_sigmoid(x):
    return jnp.minimum(x, 0.0) - jnp.log1p(jnp.exp(-jnp.abs(x)))


def _split3(x):
    hi = x.astype(BF16)
    r1 = x - hi.astype(F32)
    mid = r1.astype(BF16)
    lo = (r1 - mid.astype(F32)).astype(BF16)
    return hi, mid, lo


def _layer_norm(t, g, b):
    mu = jnp.mean(t, axis=-1, keepdims=True)
    c = t - mu
    var = jnp.mean(c * c, axis=-1, keepdims=True)
    return c * lax.rsqrt(var + LN_EPS) * g + b


def _params(sem):
    return pltpu.CompilerParams(dimension_semantics=sem, vmem_limit_bytes=VMEM_LIMIT)


def _const_spec(shape):
    nd = len(shape)
    return pl.BlockSpec(shape, lambda *_: (0,) * nd)


def _resident_spec(shape):
    nd = len(shape)
    return pl.BlockSpec(shape, lambda *_: (0,) * nd, pipeline_mode=pl.Buffered(1))


DEC_PAGES = 16


class _DecodeJob(NamedTuple):
    page_flat: jax.Array
    kc: jax.Array
    vc: jax.Array
    lfc: jax.Array
    qt: jax.Array
    ktn: jax.Array
    vtn: jax.Array
    lfn: jax.Array
    first: int
    count: int

    @property
    def pages_per_seq(self):
        return self.page_flat.shape[0] // self.qt.shape[-1]

    @property
    def steps_per_seq(self):
        return self.pages_per_seq // DEC_PAGES


class _DecodeRefs(NamedTuple):
    k: tuple
    v: tuple
    lf: tuple
    qt: object
    ktn: object
    vtn: object
    lfn: object
    out: object
    qb: object
    m: object
    l: object
    acc: object
    carry: object


def _dec_column(x, e):
    lane = lax.broadcasted_iota(jnp.int32, x.shape, 1)
    col = jnp.sum(jnp.where(lane == e, x, 0.0), axis=1, keepdims=True)
    return jnp.broadcast_to(col, (x.shape[0], LANES))


def _dec_begin(d, e, jj, first_step):
    @pl.when(first_step)
    def _():
        d.out[...] = jnp.zeros_like(d.out)

    @pl.when(jj == 0)
    def _():
        for h in range(N_HEADS):
            d.qb[h] = _dec_column(d.qt[h], e)
        d.m[...] = jnp.full_like(d.m, NEG_BIG)
        d.l[...] = jnp.zeros_like(d.l)
        d.acc[...] = jnp.zeros_like(d.acc)
        d.carry[...] = _dec_column(d.lfn[...], e)


def _dec_main(d):
    pg = len(d.k)
    lane = lax.broadcasted_iota(jnp.int32, (N_HEADS, LANES), 1)
    sub = lax.broadcasted_iota(jnp.int32, (N_HEADS, LANES), 0)
    carry = d.carry[...]
    logits = [None] * pg
    for i in reversed(range(pg)):
        lf = d.lf[i][0]
        sx = lf
        k = 1
        while k < LANES:
            sx = sx + jnp.where(lane < LANES - k, pltpu.roll(sx, LANES - k, 1), 0.0)
            k *= 2
        logits[i] = carry + (sx - lf)
        carry = carry + jnp.broadcast_to(sx[:, 0:1], sx.shape)
    d.carry[...] = carry

    for i in range(pg):
        s = logits[i]
        for h in range(N_HEADS):
            r = jnp.sum(d.k[i][0, h] * d.qb[h], axis=0, keepdims=True)
            s = s + jnp.where(sub == h, jnp.broadcast_to(r, s.shape), 0.0)
        logits[i] = s
    m_old = d.m[...]
    m_new = m_old
    for s in logits:
        m_new = jnp.maximum(m_new, s)
    alpha = jnp.exp(m_old - m_new)
    probs = [jnp.exp(s - m_new) for s in logits]
    d.l[...] = alpha * d.l[...] + functools.reduce(lambda x, y: x + y, probs)
    d.m[...] = m_new
    for h in range(N_HEADS):
        acc = d.acc[h] * jnp.broadcast_to(alpha[h:h + 1, :], (HEAD_DIM, LANES))
        for i in range(pg):
            acc = acc + d.v[i][0, h] * jnp.broadcast_to(probs[i][h:h + 1, :], (HEAD_DIM, LANES))
        d.acc[h] = acc


def _dec_end(d, e, jj, steps_per_seq):
    @pl.when(jj == steps_per_seq - 1)
    def _():
        sub = lax.broadcasted_iota(jnp.int32, (N_HEADS, LANES), 0)
        m = d.m[...]
        l = d.l[...]
        s_new = jnp.zeros((N_HEADS, LANES), F32)
        for h in range(N_HEADS):
            r = jnp.sum(_dec_column(d.ktn[h], e) * d.qb[h], axis=0, keepdims=True)
            s_new = s_new + jnp.where(sub == h, jnp.broadcast_to(r, s_new.shape), 0.0)
        m_tot = jnp.maximum(jnp.broadcast_to(jnp.max(m, axis=1, keepdims=True), m.shape), s_new)
        w = jnp.exp(m - m_tot)
        p_new = jnp.exp(s_new - m_tot)
        l_tot = jnp.broadcast_to(jnp.sum(l * w, axis=1, keepdims=True), l.shape) + p_new
        inv = 1.0 / l_tot
        olane = lax.broadcasted_iota(jnp.int32, (HEAD_DIM, d.out.shape[-1]), 1)
        for h in range(N_HEADS):
            wh = jnp.broadcast_to(w[h:h + 1, :], (HEAD_DIM, LANES))
            num = jnp.sum(d.acc[h] * wh, axis=1, keepdims=True)
            num = jnp.broadcast_to(num, (HEAD_DIM, LANES)) + _dec_column(d.vtn[h], e) * jnp.broadcast_to(
                p_new[h:h + 1, :], (HEAD_DIM, LANES))
            val = num * jnp.broadcast_to(inv[h:h + 1, :], (HEAD_DIM, LANES))
            d.out[h] = jnp.where(olane == e, val, d.out[h])


def _tile_call(body, *, grid, in_specs, out_specs, out_shape, scratch_shapes, args, name, job=None):
    sem = ("arbitrary",) * len(grid)
    if job is None:
        return pl.pallas_call(
            functools.partial(body, overlap=lambda: None), grid=grid, in_specs=in_specs, out_specs=out_specs,
            out_shape=out_shape, scratch_shapes=scratch_shapes, compiler_params=_params(sem), name=name)(*args)

    nb = job.qt.shape[-1]
    page = job.kc.shape[-1]
    assert page == LANES and nb == LANES
    pps, sps = job.pages_per_seq, job.steps_per_seq
    n_steps = 1
    for g in grid:
        n_steps *= g
    assert n_steps == job.count * sps, (n_steps, job.count, sps)

    def locate(ids):
        step = ids[0]
        for g, i in zip(grid[1:], ids[1:]):
            step = step * g + i
        return job.first + step // sps, step % sps, step == 0

    def page_index(i, trailing):
        def index_map(*a):
            e, jj, _ = locate(a[:-1])
            return (a[-1][e * pps + (sps - 1 - jj) * DEC_PAGES + i],) + (0,) * trailing
        return index_map

    dec_in_specs = ([pl.BlockSpec((1, N_HEADS, HEAD_DIM, page), page_index(i, 3)) for i in range(DEC_PAGES)] * 2
                    + [pl.BlockSpec((1, N_HEADS, page), page_index(i, 2)) for i in range(DEC_PAGES)]
                    + [_const_spec((N_HEADS, HEAD_DIM, nb))] * 3 + [_const_spec((N_HEADS, nb))])
    dec_args = ([job.kc] * DEC_PAGES + [job.vc] * DEC_PAGES + [job.lfc] * DEC_PAGES
                + [job.qt, job.ktn, job.vtn, job.lfn])
    dec_scratch = [pltpu.VMEM((N_HEADS, HEAD_DIM, LANES), F32), pltpu.VMEM((N_HEADS, LANES), F32),
                   pltpu.VMEM((N_HEADS, LANES), F32), pltpu.VMEM((N_HEADS, HEAD_DIM, LANES), F32),
                   pltpu.VMEM((N_HEADS, LANES), F32)]
    n_in, n_out, n_scr = len(in_specs), len(out_specs), len(scratch_shapes)

    def fused(pt_ref, *refs):
        del pt_ref
        host_in, dec_in = refs[:n_in], refs[n_in:n_in + len(dec_in_specs)]
        rest = refs[n_in + len(dec_in_specs):]
        host_out, dec_out, scr = rest[:n_out], rest[n_out], rest[n_out + 1:]
        d = _DecodeRefs(dec_in[:DEC_PAGES], dec_in[DEC_PAGES:2 * DEC_PAGES], dec_in[2 * DEC_PAGES:3 * DEC_PAGES],
                        *dec_in[3 * DEC_PAGES:], dec_out, *scr[n_scr:])
        e, jj, first_step = locate([pl.program_id(a) for a in range(len(grid))])

        def overlap():
            _dec_begin(d, e, jj, first_step)
            _dec_main(d)

        body(*host_in, *host_out, *scr[:n_scr], overlap=overlap)
        _dec_end(d, e, jj, sps)

    grid_spec = pltpu.PrefetchScalarGridSpec(
        num_scalar_prefetch=1, grid=grid, in_specs=list(in_specs) + dec_in_specs,
        out_specs=tuple(out_specs) + (_const_spec((N_HEADS, HEAD_DIM, nb)),),
        scratch_shapes=list(scratch_shapes) + dec_scratch)
    return pl.pallas_call(
        fused, grid_spec=grid_spec,
        out_shape=tuple(out_shape) + (jax.ShapeDtypeStruct((N_HEADS, HEAD_DIM, nb), F32),),
        compiler_params=_params(sem), name=name)(job.page_flat, *args, *dec_args)


def _inproj_prompt_kernel(x_ref, w1_ref, wt_ref, bfr_ref, bfc_ref, tri_ref, place_ref,
                          u_ref, kaug_ref, qaug_ref, kt_ref, vt_ref, vtb_ref, lft_ref, carry_ref, *, tm, overlap):
    @pl.when(pl.program_id(1) == 0)
    def _():
        carry_ref[...] = jnp.zeros_like(carry_ref)

    overlap()
    xb = x_ref[0].astype(BF16)
    z1 = _dot(xb, w1_ref[...])
    u_ref[0] = z1[:, :D_SSM]
    kpad = z1[:, D_SSM:D_SSM + N_HEADS * KAUG]
    f_logit = z1[:, D_SSM + N_HEADS * KAUG:]

    lane = lax.broadcasted_iota(jnp.int32, (tm, LANES), 1)
    lf = jnp.where(lane < N_HEADS, _log_sigmoid(f_logit + bfr_ref[...]), 0.0)
    hi, mid, lo = _split3(lf)
    cum = _dot(tri_ref[...], jnp.concatenate([hi, mid, lo], axis=1))
    fcum = (cum[:, :LANES] + cum[:, LANES:2 * LANES]) + cum[:, 2 * LANES:] + carry_ref[0:1, :]
    carry_ref[...] = jnp.broadcast_to(fcum[tm - 1:tm, :], carry_ref.shape)
    nhi, nmid, nlo = _split3(fcum * (-LOG2E))
    kaug = kpad + _dot(jnp.concatenate([nhi, nmid, nlo], axis=1), place_ref[...])
    kaug = kaug.astype(BF16)
    for h in range(N_HEADS):
        kaug_ref[0, h] = kaug[:, h * KAUG:(h + 1) * KAUG]

    zt = _dot_nt(wt_ref[...], xb)
    row = lax.broadcasted_iota(jnp.int32, (KAUG - HEAD_DIM, tm), 0)
    ones_rows = jnp.where(row < 3, 1.0, 0.0).astype(BF16)
    for h in range(N_HEADS):
        qaug_ref[0, h, 0:HEAD_DIM, :] = (zt[h * HEAD_DIM:(h + 1) * HEAD_DIM, :] * LOG2E).astype(BF16)
        qaug_ref[0, h, HEAD_DIM:KAUG, :] = ones_rows
    kt_ref[0] = zt[D_ATT:2 * D_ATT, :]
    vt = zt[2 * D_ATT:3 * D_ATT, :]
    vt_ref[0] = vt
    vrow = lax.broadcasted_iota(jnp.int32, (VAUG - HEAD_DIM, tm), 0)
    one_row = jnp.where(vrow == 0, 1.0, 0.0).astype(BF16)
    for h in range(N_HEADS):
        vtb_ref[0, h, 0:HEAD_DIM, :] = vt[h * HEAD_DIM:(h + 1) * HEAD_DIM, :].astype(BF16)
        vtb_ref[0, h, HEAD_DIM:VAUG, :] = one_row
    lft_ref[0] = _log_sigmoid(zt[3 * D_ATT:3 * D_ATT + N_HEADS, :] + bfc_ref[:, 0:1])


def _inproj_prompt(x, w1, wt, bf_row, bf_col, tri, place, tm, job=None):
    bsz, seq, _ = x.shape
    out_shape = (
        jax.ShapeDtypeStruct((bsz, seq, D_SSM), F32),
        jax.ShapeDtypeStruct((bsz, N_HEADS, seq, KAUG), BF16),
        jax.ShapeDtypeStruct((bsz, N_HEADS, KAUG, seq), BF16),
        jax.ShapeDtypeStruct((bsz, D_ATT, seq), F32),
        jax.ShapeDtypeStruct((bsz, D_ATT, seq), F32),
        jax.ShapeDtypeStruct((bsz, N_HEADS, VAUG, seq), BF16),
        jax.ShapeDtypeStruct((bsz, N_HEADS, seq), F32),
    )
    in_specs = [
        pl.BlockSpec((1, tm, D_MODEL), lambda b, l, *_: (b, l, 0)),
        _const_spec(w1.shape), _const_spec(wt.shape), _const_spec(bf_row.shape),
        _const_spec(bf_col.shape), _const_spec(tri.shape), _const_spec(place.shape),
    ]
    out_specs = (
        pl.BlockSpec((1, tm, D_SSM), lambda b, l, *_: (b, l, 0)),
        pl.BlockSpec((1, N_HEADS, tm, KAUG), lambda b, l, *_: (b, 0, l, 0)),
        pl.BlockSpec((1, N_HEADS, KAUG, tm), lambda b, l, *_: (b, 0, 0, l)),
        pl.BlockSpec((1, D_ATT, tm), lambda b, l, *_: (b, 0, l)),
        pl.BlockSpec((1, D_ATT, tm), lambda b, l, *_: (b, 0, l)),
        pl.BlockSpec((1, N_HEADS, VAUG, tm), lambda b, l, *_: (b, 0, 0, l)),
        pl.BlockSpec((1, N_HEADS, tm), lambda b, l, *_: (b, 0, l)),
    )
    return _tile_call(
        functools.partial(_inproj_prompt_kernel, tm=tm),
        grid=(bsz, seq // tm), in_specs=in_specs, out_specs=out_specs, out_shape=out_shape,
        scratch_shapes=[pltpu.VMEM((SUBLANES, LANES), F32)],
        args=(x, w1, wt, bf_row, bf_col, tri, place), name="inproj_prompt", job=job)


def _ssm_out(h_re, h_im, u, cmat_ref, d_ref, wglu_ref, bglu_ref):
    hb = jnp.concatenate([h_re.astype(BF16), h_im.astype(BF16)], axis=1)
    y = _dot(hb, cmat_ref[...]) + d_ref[...] * u
    y = jax.nn.gelu(y)
    return y * jax.nn.sigmoid(_dot(y.astype(BF16), wglu_ref[...]) + bglu_ref[...])


def _ssm_prompt_kernel(u_ref, bmat_ref, cmat_ref, tab_ref, d_ref, wglu_ref, bglu_ref,
                       y_ref, hre_ref, him_ref, bre, bim, cre, cim, *, ts, lc, overlap):
    @pl.when(pl.program_id(1) == 0)
    def _():
        cre[...] = jnp.zeros_like(cre)
        cim[...] = jnp.zeros_like(cim)

    overlap()
    u = u_ref[0]
    bu = _dot(u.astype(BF16), bmat_ref[...])
    bre[...] = bu[:, :N_STATE]
    bim[...] = bu[:, N_STATE:]

    for c in range(N_STATE // lc):
        cols = slice(c * lc, (c + 1) * lc)

        def slab(i, carry, cols=cols):
            pr, pi = carry
            r0 = pl.multiple_of(i * SUBLANES, SUBLANES)
            xr = bre[pl.ds(r0, SUBLANES), cols]
            xi = bim[pl.ds(r0, SUBLANES), cols]
            for s, k in enumerate((1, 2, 4)):
                mr = tab_ref[s, :, cols]
                mi = tab_ref[3 + s, :, cols]
                sr = pltpu.roll(xr, k, 0)
                si = pltpu.roll(xi, k, 0)
                xr, xi = xr + (mr * sr - mi * si), xi + (mr * si + mi * sr)
            qr = tab_ref[6, :, cols]
            qi = tab_ref[7, :, cols]
            xr, xi = xr + (qr * pr - qi * pi), xi + (qr * pi + qi * pr)
            bre[pl.ds(r0, SUBLANES), cols] = xr
            bim[pl.ds(r0, SUBLANES), cols] = xi
            return (jnp.broadcast_to(xr[SUBLANES - 1:SUBLANES, :], xr.shape),
                    jnp.broadcast_to(xi[SUBLANES - 1:SUBLANES, :], xi.shape))

        lr, li = lax.fori_loop(0, ts // SUBLANES, slab, (cre[:, cols], cim[:, cols]))
        cre[:, cols] = lr
        cim[:, cols] = li

    y_ref[0] = _ssm_out(bre[...], bim[...], u, cmat_ref, d_ref, wglu_ref, bglu_ref).astype(y_ref.dtype)
    hre_ref[0] = cre[...]
    him_ref[0] = cim[...]


def _ssm_prompt(u, bmat, cmat, tab, d_row, wglu, bglu, ts, lc=512, job=None):
    bsz, seq, _ = u.shape
    out_shape = (
        jax.ShapeDtypeStruct((bsz, seq, D_SSM), BF16),
        jax.ShapeDtypeStruct((bsz, SUBLANES, N_STATE), F32),
        jax.ShapeDtypeStruct((bsz, SUBLANES, N_STATE), F32),
    )
    in_specs = [
        pl.BlockSpec((1, ts, D_SSM), lambda b, l, *_: (b, l, 0)),
        _const_spec(bmat.shape), _const_spec(cmat.shape), _const_spec(tab.shape),
        _const_spec(d_row.shape), _const_spec(wglu.shape), _const_spec(bglu.shape),
    ]
    out_specs = (
        pl.BlockSpec((1, ts, D_SSM), lambda b, l, *_: (b, l, 0)),
        pl.BlockSpec((1, SUBLANES, N_STATE), lambda b, l, *_: (b, 0, 0)),
        pl.BlockSpec((1, SUBLANES, N_STATE), lambda b, l, *_: (b, 0, 0)),
    )
    return _tile_call(
        functools.partial(_ssm_prompt_kernel, ts=ts, lc=lc),
        grid=(bsz, seq // ts), in_specs=in_specs, out_specs=out_specs, out_shape=out_shape,
        scratch_shapes=[pltpu.VMEM((ts, N_STATE), F32), pltpu.VMEM((ts, N_STATE), F32),
                        pltpu.VMEM((SUBLANES, N_STATE), F32), pltpu.VMEM((SUBLANES, N_STATE), F32)],
        args=(u, bmat, cmat, tab, d_row, wglu, bglu), name="ssm_prompt", job=job)


def _ssm_sample_kernel(u_ref, h0r_ref, h0i_ref, ar_ref, ai_ref, bmat_ref, cmat_ref, d_ref, wglu_ref,
                       bglu_ref, y_ref, hr_ref, hi_ref):
    u = u_ref[...]
    bu = _dot(u.astype(BF16), bmat_ref[...])
    ar = ar_ref[...]
    ai = ai_ref[...]
    h0r = h0r_ref[...]
    h0i = h0i_ref[...]
    hr = bu[:, :N_STATE] + (ar * h0r - ai * h0i)
    hi = bu[:, N_STATE:] + (ar * h0i + ai * h0r)
    hr_ref[...] = hr
    hi_ref[...] = hi
    y_ref[...] = _ssm_out(hr, hi, u, cmat_ref, d_ref, wglu_ref, bglu_ref).astype(y_ref.dtype)


def _ssm_sample(u, h0r, h0i, a_re, a_im, bmat, cmat, d_row, wglu, bglu):
    n = u.shape[0]
    args = (u, h0r, h0i, a_re, a_im, bmat, cmat, d_row, wglu, bglu)
    return pl.pallas_call(
        _ssm_sample_kernel,
        grid=(1,), in_specs=[_const_spec(a.shape) for a in args],
        out_specs=(_const_spec((n, D_SSM)), _const_spec((n, N_STATE)), _const_spec((n, N_STATE))),
        out_shape=(jax.ShapeDtypeStruct((n, D_SSM), BF16), jax.ShapeDtypeStruct((n, N_STATE), F32),
                   jax.ShapeDtypeStruct((n, N_STATE), F32)),
        compiler_params=_params(("arbitrary",)),
        name="ssm_sample",
    )(*args)


def _attn_prompt_kernel(qaug_ref, kaug_ref, vt_ref, o_ref, s_ref, *, tq, heads):
    qi = pl.program_id(2)

    def qk(kb, slot, hh):
        k0 = pl.multiple_of(kb * tq, tq)
        s_ref[slot, hh] = _dot(kaug_ref[0, hh, pl.ds(k0, tq), :], qaug_ref[0, hh])

    def softmax_pv(kb, slot, hh, state, masked):
        m, acc = state
        k0 = pl.multiple_of(kb * tq, tq)
        s = s_ref[slot, hh]
        if masked:
            krow = lax.broadcasted_iota(jnp.int32, (tq, tq), 0)
            qcol = lax.broadcasted_iota(jnp.int32, (tq, tq), 1)
            s = jnp.where(krow <= qcol, s, NEG_BIG)
        m_new = jnp.maximum(m, jnp.max(s, axis=0, keepdims=True))
        p = jnp.exp2(s - m_new).astype(BF16)
        pv = _dot(vt_ref[0, hh, :, pl.ds(k0, tq)], p)
        return m_new, jnp.exp2(m - m_new) * acc + pv

    def step(kb, cur, states):
        out = []
        qk(kb + 1, 1 - cur, 0)
        for hh in range(heads):
            if hh + 1 < heads:
                qk(kb + 1, 1 - cur, hh + 1)
            out.append(softmax_pv(kb, cur, hh, states[hh], False))
        return tuple(out)

    def finish(cur, states):
        outs = []
        for hh in range(heads):
            _, acc = softmax_pv(qi, cur, hh, states[hh], True)
            outs.append((acc[0:HEAD_DIM] / acc[HEAD_DIM:HEAD_DIM + 1]).T)
        o_ref[0] = jnp.concatenate(outs, axis=1).astype(o_ref.dtype)

    for hh in range(heads):
        qk(0, 0, hh)
    init = tuple((jnp.full((1, tq), NEG_BIG, F32), jnp.zeros((VAUG, tq), F32)) for _ in range(heads))
    states = lax.fori_loop(0, qi // 2, lambda i, st: step(2 * i + 1, 1, step(2 * i, 0, st)), init)
    odd = lax.rem(qi, 2) == 1

    @pl.when(odd)
    def _():
        finish(1, step(qi - 1, 0, states))

    @pl.when(jnp.logical_not(odd))
    def _():
        finish(0, states)


def _attn_prompt(qaug, kaug, vtb, tq, heads):
    bsz, _, _, seq = qaug.shape
    width = heads * HEAD_DIM
    in_specs = [
        pl.BlockSpec((1, heads, KAUG, tq), lambda b, h, q: (b, h, 0, q)),
        pl.BlockSpec((1, heads, seq, KAUG), lambda b, h, q: (b, h, 0, 0)),
        pl.BlockSpec((1, heads, VAUG, seq), lambda b, h, q: (b, h, 0, 0)),
    ]
    return pl.pallas_call(
        functools.partial(_attn_prompt_kernel, tq=tq, heads=heads),
        grid=(bsz, N_HEADS // heads, seq // tq),
        in_specs=in_specs,
        out_specs=pl.BlockSpec((1, tq, width), lambda b, h, q: (b, q, h)),
        out_shape=jax.ShapeDtypeStruct((bsz, seq, D_ATT), BF16),
        scratch_shapes=[pltpu.VMEM((2, heads, tq, tq), F32)],
        compiler_params=_params(("arbitrary", "arbitrary", "arbitrary")),
        name="attn_prompt",
    )(qaug, kaug, vtb)


def _mix_kernel(x_ref, ys_ref, ya_ref, wgs_ref, wga_ref, wbs_ref, wba_ref, wo_ref, g_ref, b_ref, o_ref, *,
                overlap):
    overlap()
    x = x_ref[0]
    xb = x.astype(BF16)
    gs = jax.nn.sigmoid(_dot(xb, wgs_ref[...]))
    ga = jax.nn.sigmoid(_dot(xb, wga_ref[...]))
    mixed = gs * _dot(ys_ref[0], wbs_ref[...]) + ga * _dot(ya_ref[0], wba_ref[...])
    t = ALPHA * x + _dot(mixed.astype(BF16), wo_ref[...])
    o_ref[0] = _layer_norm(t, g_ref[...], b_ref[...])


def _mix(x, ys, ya, wgs, wga, wbs, wba, wo, g, b, tm, job=None):
    bsz, seq, _ = x.shape
    ws = (wgs, wga, wbs, wba, wo, g, b)
    in_specs = [
        pl.BlockSpec((1, tm, D_MODEL), lambda i, l, *_: (i, l, 0)),
        pl.BlockSpec((1, tm, D_SSM), lambda i, l, *_: (i, l, 0)),
        pl.BlockSpec((1, tm, D_ATT), lambda i, l, *_: (i, l, 0)),
    ] + [_const_spec(w.shape) for w in ws]
    return _tile_call(
        _mix_kernel, grid=(bsz, seq // tm), in_specs=in_specs,
        out_specs=(pl.BlockSpec((1, tm, D_MODEL), lambda i, l, *_: (i, l, 0)),),
        out_shape=(jax.ShapeDtypeStruct((bsz, seq, D_MODEL), F32),),
        scratch_shapes=[], args=(x, ys, ya, *ws), name="mix_ln1", job=job)


def _ffn_tail(x, a, am1, am2, bgate, cw_ref, cb_ref, wdn_ref, g_ref, b_ref, wpg_ref, bpg_ref, wple_ref, p):
    c = cb_ref[...] + cw_ref[0:1, :] * am2 + cw_ref[1:2, :] * am1 + cw_ref[2:3, :] * a
    hcat = (jax.nn.gelu(c) * bgate).astype(BF16)
    x2 = _layer_norm(ALPHA * x + _dot(hcat, wdn_ref[...]), g_ref[...], b_ref[...])
    gate = jax.nn.sigmoid(_dot(x2.astype(BF16), wpg_ref[...]) + bpg_ref[...])
    return x2 + gate * _dot(p.astype(BF16), wple_ref[...])


def _ffn_prompt_kernel(x_ref, p_ref, wup_ref, cw_ref, cb_ref, wdn_ref, g_ref, b_ref, wpg_ref, bpg_ref,
                       wple_ref, o_ref, tail_ref, prev_ref, *, tm, overlap):
    @pl.when(pl.program_id(1) == 0)
    def _():
        prev_ref[...] = jnp.zeros_like(prev_ref)

    overlap()
    x = x_ref[0]
    up = _dot(x.astype(BF16), wup_ref[...])
    a = up[:, :D_FF]
    row = lax.broadcasted_iota(jnp.int32, (tm, D_FF), 0)
    p1 = jnp.broadcast_to(prev_ref[SUBLANES - 1:SUBLANES, :], (tm, D_FF))
    p2 = jnp.broadcast_to(prev_ref[SUBLANES - 2:SUBLANES - 1, :], (tm, D_FF))
    am1 = jnp.where(row == 0, p1, pltpu.roll(a, 1, 0))
    am2 = jnp.where(row == 0, p2, jnp.where(row == 1, p1, pltpu.roll(a, 2, 0)))
    tail = a[tm - SUBLANES:, :]
    prev_ref[...] = tail
    tail_ref[0] = tail
    o_ref[0] = _ffn_tail(x, a, am1, am2, up[:, D_FF:], cw_ref, cb_ref, wdn_ref, g_ref, b_ref,
                         wpg_ref, bpg_ref, wple_ref, p_ref[0])


def _ffn_sample_kernel(x_ref, p_ref, s0_ref, s1_ref, wup_ref, cw_ref, cb_ref, wdn_ref, g_ref, b_ref,
                       wpg_ref, bpg_ref, wple_ref, o_ref, a_ref):
    x = x_ref[...]
    up = _dot(x.astype(BF16), wup_ref[...])
    a = up[:, :D_FF]
    a_ref[...] = a
    o_ref[...] = _ffn_tail(x, a, s1_ref[...], s0_ref[...], up[:, D_FF:], cw_ref, cb_ref, wdn_ref, g_ref,
                           b_ref, wpg_ref, bpg_ref, wple_ref, p_ref[...])


def _ffn_prompt(x, p, ws, tm, job=None):
    bsz, seq, _ = x.shape
    in_specs = [
        pl.BlockSpec((1, tm, D_MODEL), lambda i, l, *_: (i, l, 0)),
        pl.BlockSpec((1, tm, p.shape[-1]), lambda i, l, *_: (i, l, 0)),
    ] + [_resident_spec(w.shape) for w in ws]
    return _tile_call(
        functools.partial(_ffn_prompt_kernel, tm=tm), grid=(bsz, seq // tm), in_specs=in_specs,
        out_specs=(pl.BlockSpec((1, tm, D_MODEL), lambda i, l, *_: (i, l, 0)),
                   pl.BlockSpec((1, SUBLANES, D_FF), lambda i, l, *_: (i, 0, 0))),
        out_shape=(jax.ShapeDtypeStruct((bsz, seq, D_MODEL), F32),
                   jax.ShapeDtypeStruct((bsz, SUBLANES, D_FF), F32)),
        scratch_shapes=[pltpu.VMEM((SUBLANES, D_FF), F32)],
        args=(x, p, *ws), name="ffn_prompt", job=job)


def _ffn_sample(x, p, s0, s1, ws):
    n = x.shape[0]
    args = (x, p, s0, s1) + tuple(ws)
    return pl.pallas_call(
        _ffn_sample_kernel, grid=(1,), in_specs=[_const_spec(a.shape) for a in args],
        out_specs=(_const_spec((n, D_MODEL)), _const_spec((n, D_FF))),
        out_shape=(jax.ShapeDtypeStruct((n, D_MODEL), F32), jax.ShapeDtypeStruct((n, D_FF), F32)),
        compiler_params=_params(("arbitrary",)),
        name="ffn_sample",
    )(*args)


def _inproj_sample_kernel(x_ref, wu_ref, wt_ref, bfc_ref, u_ref, qt_ref, kt_ref, vt_ref, lft_ref):
    xb = x_ref[...].astype(BF16)
    u_ref[...] = _dot(xb, wu_ref[...])
    zt = _dot_nt(wt_ref[...], xb)
    qt_ref[...] = zt[0:D_ATT, :]
    kt_ref[...] = zt[D_ATT:2 * D_ATT, :]
    vt_ref[...] = zt[2 * D_ATT:3 * D_ATT, :]
    lft_ref[...] = _log_sigmoid(zt[3 * D_ATT:3 * D_ATT + N_HEADS, :] + bfc_ref[:, 0:1])


def _inproj_sample(x, wu, wt, bf_col):
    n = x.shape[0]
    args = (x, wu, wt, bf_col)
    shp = lambda r: jax.ShapeDtypeStruct((r, n), F32)
    return pl.pallas_call(
        _inproj_sample_kernel, grid=(1,), in_specs=[_const_spec(a.shape) for a in args],
        out_specs=(_const_spec((n, D_SSM)), _const_spec((D_ATT, n)), _const_spec((D_ATT, n)),
                   _const_spec((D_ATT, n)), _const_spec((N_HEADS, n))),
        out_shape=(jax.ShapeDtypeStruct((n, D_SSM), F32), shp(D_ATT), shp(D_ATT), shp(D_ATT), shp(N_HEADS)),
        compiler_params=_params(("arbitrary",)),
        name="inproj_sample",
    )(*args)


def _decode_only(job):
    def body(*, overlap):
        overlap()

    return _tile_call(body, grid=(job.count, job.steps_per_seq), in_specs=[], out_specs=(), out_shape=(),
                      scratch_shapes=[], args=(), name="decode_attn", job=job)[0]


def _discretize(a_re, a_im, log_dt, b_re, b_im):
    dt = jnp.exp(log_dt)[:, None]
    mag = jnp.exp(dt * a_re)
    ang = dt * a_im
    e_re = mag * jnp.cos(ang)
    e_im = mag * jnp.sin(ang)
    nr = e_re - 1.0
    ni = e_im
    den = a_re * a_re + a_im * a_im
    z_re = ((nr * a_re + ni * a_im) / den)[..., None]
    z_im = ((ni * a_re - nr * a_im) / den)[..., None]
    return e_re, e_im, z_re * b_re - z_im * b_im, z_re * b_im + z_im * b_re


def _block_diag(blocks):
    g, r, c = blocks.shape
    eye = jnp.eye(g, dtype=blocks.dtype)
    return (eye[:, None, :, None] * blocks[:, :, None, :]).reshape(g * r, g * c)


def _cmul(ar, ai, br, bi):
    return ar * br - ai * bi, ar * bi + ai * br


def _scan_tables(e_re, e_im):
    a1r, a1i = e_re.reshape(1, N_STATE), e_im.reshape(1, N_STATE)
    a2r, a2i = _cmul(a1r, a1i, a1r, a1i)
    a4r, a4i = _cmul(a2r, a2i, a2r, a2i)
    rows = jnp.arange(SUBLANES)[:, None]
    tabs_r, tabs_i = [], []
    for k, (pr, pi) in zip((1, 2, 4), ((a1r, a1i), (a2r, a2i), (a4r, a4i))):
        tabs_r.append(jnp.where(rows >= k, pr, 0.0))
        tabs_i.append(jnp.where(rows >= k, pi, 0.0))
    pw_r, pw_i = [a1r], [a1i]
    for _ in range(SUBLANES - 1):
        nr, ni = _cmul(pw_r[-1], pw_i[-1], a1r, a1i)
        pw_r.append(nr)
        pw_i.append(ni)
    return jnp.stack(tabs_r + tabs_i + [jnp.concatenate(pw_r, 0), jnp.concatenate(pw_i, 0)])


def kernel(x_prompt, x_sample, p_prompt, p_sample, cache_k, cache_v, cache_logf, page_table, state_ssm_re, state_ssm_im, state_conv, w_in, b_f, ssm_a_re, ssm_a_im, ssm_log_dt, ssm_b_re, ssm_b_im, ssm_c_re, ssm_c_im, ssm_d, w_glu, b_glu, w_br_ssm, w_br_att, w_o, ln1_g, ln1_b, w_up, conv_w, conv_b, w_down, ln2_g, ln2_b, w_ple_gate, b_ple_gate, w_ple):
    assert w_in.shape[0] == DEPTH
    bsz, seq, _ = x_prompt.shape
    nb = x_sample.shape[0]
    tm = 256

    win = w_in[0]
    o = [0, D_SSM, D_SSM + D_ATT, D_SSM + 2 * D_ATT, D_SSM + 3 * D_ATT, D_SSM + 3 * D_ATT + N_HEADS]
    wu, wq, wk, wv, wf = (win[:, o[i]:o[i + 1]] for i in range(5))
    wgs = win[:, o[5]:o[5] + D_MODEL].astype(BF16)
    wga = win[:, o[5] + D_MODEL:].astype(BF16)
    scale = HEAD_DIM ** -0.5
    wk_pad = jnp.pad(wk.reshape(D_MODEL, N_HEADS, HEAD_DIM), ((0, 0), (0, 0), (0, KAUG - HEAD_DIM)))
    wf_pad = jnp.pad(wf, ((0, 0), (0, LANES - N_HEADS)))
    w1 = jnp.concatenate([wu, wk_pad.reshape(D_MODEL, N_HEADS * KAUG), wf_pad], axis=1).astype(BF16)
    wt = jnp.concatenate([wq * scale, wk, wv, wf], axis=1).T.astype(BF16)
    wu_b = wu.astype(BF16)
    bf_row = jnp.pad(b_f, ((0, 0), (0, LANES - N_HEADS)))
    bf_col = jnp.broadcast_to(b_f[0][:, None], (N_HEADS, LANES))
    tri = jnp.tri(tm, dtype=BF16)
    pidx = jnp.arange(3 * LANES)
    place = jnp.zeros((3 * LANES, N_HEADS * KAUG), BF16).at[
        pidx, (pidx % LANES) * KAUG + HEAD_DIM + pidx // LANES].set(
        jnp.where(pidx % LANES < N_HEADS, 1.0, 0.0).astype(BF16), mode="drop")

    e_re, e_im, bb_re, bb_im = _discretize(ssm_a_re[0], ssm_a_im[0], ssm_log_dt[0], ssm_b_re[0], ssm_b_im[0])
    bmat = jnp.concatenate([_block_diag(jnp.swapaxes(bb_re, 1, 2)),
                            _block_diag(jnp.swapaxes(bb_im, 1, 2))], axis=1).astype(BF16)
    cmat = jnp.concatenate([_block_diag(jnp.swapaxes(ssm_c_re[0], 1, 2)),
                            _block_diag(jnp.swapaxes(-ssm_c_im[0], 1, 2))], axis=0).astype(BF16)
    tab = _scan_tables(e_re, e_im)
    a_re_row, a_im_row = e_re.reshape(1, N_STATE), e_im.reshape(1, N_STATE)
    wglu = w_glu[0].astype(BF16)
    mix_ws = (wgs, wga, w_br_ssm[0].astype(BF16), w_br_att[0].astype(BF16), w_o[0].astype(BF16), ln1_g, ln1_b)
    ffn_ws = (w_up[0].astype(BF16), conv_w[0], conv_b, w_down[0].astype(BF16), ln2_g, ln2_b,
              w_ple_gate[0].astype(BF16), b_ple_gate, w_ple[0].astype(BF16))

    xs = x_sample[:, 0]
    us, qts, kts, vts, lfts = _inproj_sample(xs, wu_b, wt, bf_col)
    kc = jnp.transpose(cache_k[0], (0, 2, 3, 1))
    vc = jnp.transpose(cache_v[0], (0, 2, 3, 1))
    lfc = jnp.transpose(cache_logf[0], (0, 2, 1))
    shp3 = (N_HEADS, HEAD_DIM, nb)
    todo = _DecodeJob(page_table.reshape(-1), kc, vc, lfc, qts.reshape(shp3), kts.reshape(shp3),
                      vts.reshape(shp3), lfts, 0, nb)
    host_steps = bsz * (seq // tm)
    decoded = []

    def hosted(call, *args, **kwargs):
        nonlocal todo
        count = host_steps // todo.steps_per_seq
        if count * todo.steps_per_seq != host_steps or count > todo.count:
            return call(*args, **kwargs)
        *outs, dec = call(*args, **kwargs, job=todo._replace(count=count))
        decoded.append(dec)
        todo = todo._replace(first=todo.first + count, count=todo.count - count)
        return outs

    u, kaug, qaug, kt, vt, vtb, lft = hosted(_inproj_prompt, x_prompt, w1, wt, bf_row, bf_col, tri, place, tm)
    ys, hre, him = hosted(_ssm_prompt, u, bmat, cmat, tab, ssm_d, wglu, b_glu, ts=tm)
    ya = _attn_prompt(qaug, kaug, vtb, tq=512, heads=4)
    x1, = hosted(_mix, x_prompt, ys, ya, *mix_ws, tm=tm)
    y_prompt, tail = hosted(_ffn_prompt, x1, p_prompt[0], ffn_ws, tm=tm)
    if todo.count:
        decoded.append(_decode_only(todo))

    k_prompt = jnp.transpose(kt.reshape(bsz, N_HEADS, HEAD_DIM, seq), (0, 3, 1, 2))[None]
    v_prompt = jnp.transpose(vt.reshape(bsz, N_HEADS, HEAD_DIM, seq), (0, 3, 1, 2))[None]
    logf_prompt = jnp.transpose(lft, (0, 2, 1))[None]
    ssm_re_prompt = hre[:, 0].reshape(1, bsz, N_GROUPS, STATE_DIM)
    ssm_im_prompt = him[:, 0].reshape(1, bsz, N_GROUPS, STATE_DIM)
    conv_prompt = tail[:, SUBLANES - (CONV_W - 1):][None]

    ys_s, hr_s, hi_s = _ssm_sample(us, state_ssm_re[0].reshape(nb, N_STATE), state_ssm_im[0].reshape(nb, N_STATE),
                                   a_re_row, a_im_row, bmat, cmat, ssm_d, wglu, b_glu)
    yat = functools.reduce(lambda x, y: x + y, decoded)
    ya_s = yat.reshape(D_ATT, nb).T.astype(BF16)
    x1_s = _mix(xs[None], ys_s[None], ya_s[None], *mix_ws, tm=nb)[0][0]
    y_s, a_s = _ffn_sample(x1_s, p_sample[0, :, 0], state_conv[0, :, 0], state_conv[0, :, 1], ffn_ws)

    y_sample = y_s[:, None, :]
    k_sample = kts.T.reshape(1, nb, 1, N_HEADS, HEAD_DIM)
    v_sample = vts.T.reshape(1, nb, 1, N_HEADS, HEAD_DIM)
    logf_sample = lfts.T.reshape(1, nb, 1, N_HEADS)
    ssm_re_sample = hr_s.reshape(1, nb, N_GROUPS, STATE_DIM)
    ssm_im_sample = hi_s.reshape(1, nb, N_GROUPS, STATE_DIM)
    conv_sample = jnp.stack([state_conv[0, :, 1], a_s], axis=1)[None]

    return (y_prompt, y_sample, k_prompt, v_prompt, logf_prompt, ssm_re_prompt, ssm_im_prompt, conv_prompt,
            k_sample, v_sample, logf_sample, ssm_re_sample, ssm_im_sample, conv_sample)
```

```python
import functools
from typing import NamedTuple

import jax
import jax.numpy as jnp
from jax import lax
from jax.experimental import pallas as pl
from jax.experimental.pallas import tpu as pltpu

F32 = jnp.float32
BF16 = jnp.bfloat16

D_MODEL = 1024
D_SSM = 512
SSM_GROUP = 16
N_GROUPS = 32
STATE_DIM = 64
N_STATE = N_GROUPS * STATE_DIM
N_HEADS = 8
HEAD_DIM = 64
D_ATT = 512
D_FF = 2816
CONV_W = 3
DEPTH = 1
ALPHA = (2 * DEPTH) ** 0.25
LN_EPS = 1e-5
NEG_BIG = -1e30
LANES = 128
SUBLANES = 8
KAUG = 128
VAUG = 80
LOG2E = 1.4426950408889634
VMEM_LIMIT = 56 * 1024 * 1024


def _dot(a, b):
    return jnp.dot(a, b, preferred_element_type=F32)


def _dot_nt(a, b):
    return lax.dot_general(a, b, (((1,), (1,)), ((), ())), preferred_element_type=F32)


def _log_sigmoid(x):
    return jnp.minimum(x, 0.0) - jnp.log1p(jnp.exp(-jnp.abs(x)))


def _split3(x):
    hi = x.astype(BF16)
    r1 = x - hi.astype(F32)
    mid = r1.astype(BF16)
    lo = (r1 - mid.astype(F32)).astype(BF16)
    return hi, mid, lo


def _layer_norm(t, g, b):
    mu = jnp.mean(t, axis=-1, keepdims=True)
    c = t - mu
    var = jnp.mean(c * c, axis=-1, keepdims=True)
    return c * lax.rsqrt(var + LN_EPS) * g + b


def _params(sem):
    return pltpu.CompilerParams(dimension_semantics=sem, vmem_limit_bytes=VMEM_LIMIT)


def _const_spec(shape):
    nd = len(shape)
    return pl.BlockSpec(shape, lambda *_: (0,) * nd)


def _resident_spec(shape):
    nd = len(shape)
    return pl.BlockSpec(shape, lambda *_: (0,) * nd, pipeline_mode=pl.Buffered(1))


DEC_PAGES = 16


class _DecodeJob(NamedTuple):
    page_flat: jax.Array
    kc: jax.Array
    vc: jax.Array
    lfc: jax.Array
    qt: jax.Array
    ktn: jax.Array
    vtn: jax.Array
    lfn: jax.Array
    first: int
    count: int

    @property
    def pages_per_seq(self):
        return self.page_flat.shape[0] // self.qt.shape[-1]

    @property
    def steps_per_seq(self):
        return self.pages_per_seq // DEC_PAGES


class _DecodeRefs(NamedTuple):
    k: tuple
    v: tuple
    lf: tuple
    qt: object
    ktn: object
    vtn: object
    lfn: object
    out: object
    qb: object
    m: object
    l: object
    acc: object
    carry: object


def _dec_column(x, e):
    lane = lax.broadcasted_iota(jnp.int32, x.shape, 1)
    col = jnp.sum(jnp.where(lane == e, x, 0.0), axis=1, keepdims=True)
    return jnp.broadcast_to(col, (x.shape[0], LANES))


def _dec_begin(d, e, jj, first_step):
    @pl.when(first_step)
    def _():
        d.out[...] = jnp.zeros_like(d.out)

    @pl.when(jj == 0)
    def _():
        for h in range(N_HEADS):
            d.qb[h] = _dec_column(d.qt[h], e)
        d.m[...] = jnp.full_like(d.m, NEG_BIG)
        d.l[...] = jnp.zeros_like(d.l)
        d.acc[...] = jnp.zeros_like(d.acc)
        d.carry[...] = _dec_column(d.lfn[...], e)


def _dec_main(d):
    pg = len(d.k)
    sub = lax.broadcasted_iota(jnp.int32, (N_HEADS, LANES), 0)
    lf_all = jnp.concatenate([d.lf[i][0] for i in range(pg)], axis=0)
    row = lax.broadcasted_iota(jnp.int32, (LANES, 2 * LANES), 0)
    col = lax.broadcasted_iota(jnp.int32, (LANES, 2 * LANES), 1)
    later = jnp.where(jnp.logical_or(row > col, col >= LANES), 1.0, 0.0).astype(BF16)
    sums = _dot(jnp.concatenate(_split3(lf_all), axis=0), later)
    n = pg * N_HEADS
    sums = (sums[0:n] + sums[n:2 * n]) + sums[2 * n:3 * n]
    carry = d.carry[...]
    logits = [None] * pg
    for i in reversed(range(pg)):
        logits[i] = carry + sums[i * N_HEADS:(i + 1) * N_HEADS, 0:LANES]
        carry = carry + sums[i * N_HEADS:(i + 1) * N_HEADS, LANES:2 * LANES]
    d.carry[...] = carry

    for h in range(N_HEADS):
        qh = d.qb[h]
        for i in range(pg):
            r = jnp.sum(d.k[i][0, h] * qh, axis=0, keepdims=True)
            logits[i] = logits[i] + jnp.where(sub == h, jnp.broadcast_to(r, logits[i].shape), 0.0)
    m_old = d.m[...]
    m_new = m_old
    for s in logits:
        m_new = jnp.maximum(m_new, s)
    alpha = jnp.exp(m_old - m_new)
    probs = [jnp.exp(s - m_new) for s in logits]
    d.l[...] = alpha * d.l[...] + functools.reduce(lambda x, y: x + y, probs)
    d.m[...] = m_new
    for h in range(N_HEADS):
        acc = d.acc[h] * jnp.broadcast_to(alpha[h:h + 1, :], (HEAD_DIM, LANES))
        for i in range(pg):
            acc = acc + d.v[i][0, h] * jnp.broadcast_to(probs[i][h:h + 1, :], (HEAD_DIM, LANES))
        d.acc[h] = acc


def _dec_end(d, e, jj, steps_per_seq):
    @pl.when(jj == steps_per_seq - 1)
    def _():
        sub = lax.broadcasted_iota(jnp.int32, (N_HEADS, LANES), 0)
        m = d.m[...]
        l = d.l[...]
        s_new = jnp.zeros((N_HEADS, LANES), F32)
        for h in range(N_HEADS):
            r = jnp.sum(_dec_column(d.ktn[h], e) * d.qb[h], axis=0, keepdims=True)
            s_new = s_new + jnp.where(sub == h, jnp.broadcast_to(r, s_new.shape), 0.0)
        m_tot = jnp.maximum(jnp.broadcast_to(jnp.max(m, axis=1, keepdims=True), m.shape), s_new)
        w = jnp.exp(m - m_tot)
        p_new = jnp.exp(s_new - m_tot)
        l_tot = jnp.broadcast_to(jnp.sum(l * w, axis=1, keepdims=True), l.shape) + p_new
        inv = 1.0 / l_tot
        olane = lax.broadcasted_iota(jnp.int32, (HEAD_DIM, d.out.shape[-1]), 1)
        for h in range(N_HEADS):
            wh = jnp.broadcast_to(w[h:h + 1, :], (HEAD_DIM, LANES))
            num = jnp.sum(d.acc[h] * wh, axis=1, keepdims=True)
            num = jnp.broadcast_to(num, (HEAD_DIM, LANES)) + _dec_column(d.vtn[h], e) * jnp.broadcast_to(
                p_new[h:h + 1, :], (HEAD_DIM, LANES))
            val = num * jnp.broadcast_to(inv[h:h + 1, :], (HEAD_DIM, LANES))
            d.out[h] = jnp.where(olane == e, val, d.out[h])


def _tile_call(body, *, grid, in_specs, out_specs, out_shape, scratch_shapes, args, name, job=None):
    sem = ("arbitrary",) * len(grid)
    if job is None:
        return pl.pallas_call(
            functools.partial(body, overlap=lambda: None), grid=grid, in_specs=in_specs, out_specs=out_specs,
            out_shape=out_shape, scratch_shapes=scratch_shapes, compiler_params=_params(sem), name=name)(*args)

    nb = job.qt.shape[-1]
    page = job.kc.shape[-1]
    assert page == LANES and nb == LANES
    pps, sps = job.pages_per_seq, job.steps_per_seq
    n_steps = 1
    for g in grid:
        n_steps *= g
    assert n_steps == job.count * sps, (n_steps, job.count, sps)

    def locate(ids):
        step = ids[0]
        for g, i in zip(grid[1:], ids[1:]):
            step = step * g + i
        return job.first + lax.div(step, jnp.int32(sps)), lax.rem(step, jnp.int32(sps)), step == 0

    def page_index(i, trailing):
        def index_map(*a):
            e, jj, _ = locate(a[:-1])
            return (a[-1][e * pps + (sps - 1 - jj) * DEC_PAGES + i],) + (0,) * trailing
        return index_map

    dec_in_specs = ([pl.BlockSpec((1, N_HEADS, HEAD_DIM, page), page_index(i, 3)) for i in range(DEC_PAGES)] * 2
                    + [pl.BlockSpec((1, N_HEADS, page), page_index(i, 2)) for i in range(DEC_PAGES)]
                    + [_const_spec((N_HEADS, HEAD_DIM, nb))] * 3 + [_const_spec((N_HEADS, nb))])
    dec_args = ([job.kc] * DEC_PAGES + [job.vc] * DEC_PAGES + [job.lfc] * DEC_PAGES
                + [job.qt, job.ktn, job.vtn, job.lfn])
    dec_scratch = [pltpu.VMEM((N_HEADS, HEAD_DIM, LANES), F32), pltpu.VMEM((N_HEADS, LANES), F32),
                   pltpu.VMEM((N_HEADS, LANES), F32), pltpu.VMEM((N_HEADS, HEAD_DIM, LANES), F32),
                   pltpu.VMEM((N_HEADS, LANES), F32)]
    n_in, n_out, n_scr = len(in_specs), len(out_specs), len(scratch_shapes)

    def fused(pt_ref, *refs):
        del pt_ref
        host_in, dec_in = refs[:n_in], refs[n_in:n_in + len(dec_in_specs)]
        rest = refs[n_in + len(dec_in_specs):]
        host_out, dec_out, scr = rest[:n_out], rest[n_out], rest[n_out + 1:]
        d = _DecodeRefs(dec_in[:DEC_PAGES], dec_in[DEC_PAGES:2 * DEC_PAGES], dec_in[2 * DEC_PAGES:3 * DEC_PAGES],
                        *dec_in[3 * DEC_PAGES:], dec_out, *scr[n_scr:])
        e, jj, first_step = locate([pl.program_id(a) for a in range(len(grid))])

        def overlap():
            _dec_begin(d, e, jj, first_step)
            _dec_main(d)

        body(*host_in, *host_out, *scr[:n_scr], overlap=overlap)
        _dec_end(d, e, jj, sps)

    grid_spec = pltpu.PrefetchScalarGridSpec(
        num_scalar_prefetch=1, grid=grid, in_specs=list(in_specs) + dec_in_specs,
        out_specs=tuple(out_specs) + (_const_spec((N_HEADS, HEAD_DIM, nb)),),
        scratch_shapes=list(scratch_shapes) + dec_scratch)
    return pl.pallas_call(
        fused, grid_spec=grid_spec,
        out_shape=tuple(out_shape) + (jax.ShapeDtypeStruct((N_HEADS, HEAD_DIM, nb), F32),),
        compiler_params=_params(sem), name=name)(job.page_flat, *args, *dec_args)


def _inproj_prompt_kernel(x_ref, w1_ref, wt_ref, bfr_ref, bfc_ref, tri_ref, place_ref,
                          u_ref, kaug_ref, qaug_ref, kt_ref, vt_ref, vtb_ref, lft_ref, carry_ref, *, tm, overlap):
    @pl.when(pl.program_id(1) == 0)
    def _():
        carry_ref[...] = jnp.zeros_like(carry_ref)

    overlap()
    xb = x_ref[0].astype(BF16)
    z1 = _dot(xb, w1_ref[...])
    u_ref[0] = z1[:, :D_SSM]
    kpad = z1[:, D_SSM:D_SSM + N_HEADS * KAUG]
    f_logit = z1[:, D_SSM + N_HEADS * KAUG:]

    lane = lax.broadcasted_iota(jnp.int32, (tm, LANES), 1)
    lf = jnp.where(lane < N_HEADS, _log_sigmoid(f_logit + bfr_ref[...]), 0.0)
    hi, mid, lo = _split3(lf)
    cum = _dot(tri_ref[...], jnp.concatenate([hi, mid, lo], axis=1))
    fcum = (cum[:, :LANES] + cum[:, LANES:2 * LANES]) + cum[:, 2 * LANES:] + carry_ref[0:1, :]
    carry_ref[...] = jnp.broadcast_to(fcum[tm - 1:tm, :], carry_ref.shape)
    nhi, nmid, nlo = _split3(fcum * (-LOG2E))
    kaug = kpad + _dot(jnp.concatenate([nhi, nmid, nlo], axis=1), place_ref[...])
    kaug = kaug.astype(BF16)
    for h in range(N_HEADS):
        kaug_ref[0, h] = kaug[:, h * KAUG:(h + 1) * KAUG]

    zt = _dot_nt(wt_ref[...], xb)
    row = lax.broadcasted_iota(jnp.int32, (KAUG - HEAD_DIM, tm), 0)
    ones_rows = jnp.where(row < 3, 1.0, 0.0).astype(BF16)
    for h in range(N_HEADS):
        qaug_ref[0, h, 0:HEAD_DIM, :] = (zt[h * HEAD_DIM:(h + 1) * HEAD_DIM, :] * LOG2E).astype(BF16)
        qaug_ref[0, h, HEAD_DIM:KAUG, :] = ones_rows
    kt_ref[0] = zt[D_ATT:2 * D_ATT, :]
    vt = zt[2 * D_ATT:3 * D_ATT, :]
    vt_ref[0] = vt
    vrow = lax.broadcasted_iota(jnp.int32, (VAUG - HEAD_DIM, tm), 0)
    one_row = jnp.where(vrow == 0, 1.0, 0.0).astype(BF16)
    for h in range(N_HEADS):
        vtb_ref[0, h, 0:HEAD_DIM, :] = vt[h * HEAD_DIM:(h + 1) * HEAD_DIM, :].astype(BF16)
        vtb_ref[0, h, HEAD_DIM:VAUG, :] = one_row
    lft_ref[0] = _log_sigmoid(zt[3 * D_ATT:3 * D_ATT + N_HEADS, :] + bfc_ref[:, 0:1])


def _inproj_prompt(x, w1, wt, bf_row, bf_col, tri, place, tm, job=None):
    bsz, seq, _ = x.shape
    out_shape = (
        jax.ShapeDtypeStruct((bsz, seq, D_SSM), F32),
        jax.ShapeDtypeStruct((bsz, N_HEADS, seq, KAUG), BF16),
        jax.ShapeDtypeStruct((bsz, N_HEADS, KAUG, seq), BF16),
        jax.ShapeDtypeStruct((bsz, D_ATT, seq), F32),
        jax.ShapeDtypeStruct((bsz, D_ATT, seq), F32),
        jax.ShapeDtypeStruct((bsz, N_HEADS, VAUG, seq), BF16),
        jax.ShapeDtypeStruct((bsz, N_HEADS, seq), F32),
    )
    in_specs = [
        pl.BlockSpec((1, tm, D_MODEL), lambda b, l, *_: (b, l, 0)),
        _const_spec(w1.shape), _const_spec(wt.shape), _const_spec(bf_row.shape),
        _const_spec(bf_col.shape), _const_spec(tri.shape), _const_spec(place.shape),
    ]
    out_specs = (
        pl.BlockSpec((1, tm, D_SSM), lambda b, l, *_: (b, l, 0)),
        pl.BlockSpec((1, N_HEADS, tm, KAUG), lambda b, l, *_: (b, 0, l, 0)),
        pl.BlockSpec((1, N_HEADS, KAUG, tm), lambda b, l, *_: (b, 0, 0, l)),
        pl.BlockSpec((1, D_ATT, tm), lambda b, l, *_: (b, 0, l)),
        pl.BlockSpec((1, D_ATT, tm), lambda b, l, *_: (b, 0, l)),
        pl.BlockSpec((1, N_HEADS, VAUG, tm), lambda b, l, *_: (b, 0, 0, l)),
        pl.BlockSpec((1, N_HEADS, tm), lambda b, l, *_: (b, 0, l)),
    )
    return _tile_call(
        functools.partial(_inproj_prompt_kernel, tm=tm),
        grid=(bsz, seq // tm), in_specs=in_specs, out_specs=out_specs, out_shape=out_shape,
        scratch_shapes=[pltpu.VMEM((SUBLANES, LANES), F32)],
        args=(x, w1, wt, bf_row, bf_col, tri, place), name="inproj_prompt", job=job)


SSM_CHUNKS = 4
CH_IN = D_SSM // SSM_CHUNKS
CH_ST = N_STATE // SSM_CHUNKS


def _ssm_bu(ub, bmat_ref, j):
    part = _dot(ub[:, j * CH_IN:(j + 1) * CH_IN], bmat_ref[j])
    return part[:, :CH_ST], part[:, CH_ST:]


def _ssm_out(h_re, h_im, u, cmat_ref, d_ref, wglu_ref, bglu_ref):
    ys = []
    for j in range(SSM_CHUNKS):
        st = slice(j * CH_ST, (j + 1) * CH_ST)
        hb = jnp.concatenate([h_re[:, st].astype(BF16), h_im[:, st].astype(BF16)], axis=1)
        ys.append(_dot(hb, cmat_ref[j]))
    y = jnp.concatenate(ys, axis=1) + d_ref[...] * u
    y = jax.nn.gelu(y)
    return y * jax.nn.sigmoid(_dot(y.astype(BF16), wglu_ref[...]) + bglu_ref[...])


def _ssm_prompt_kernel(u_ref, bmat_ref, cmat_ref, tab_ref, d_ref, wglu_ref, bglu_ref,
                       y_ref, hre_ref, him_ref, bre, bim, cre, cim, *, ts, lc, overlap):
    @pl.when(pl.program_id(1) == 0)
    def _():
        cre[...] = jnp.zeros_like(cre)
        cim[...] = jnp.zeros_like(cim)

    overlap()
    u = u_ref[0]
    ub = u.astype(BF16)
    for j in range(SSM_CHUNKS):
        st = slice(j * CH_ST, (j + 1) * CH_ST)
        bre[:, st], bim[:, st] = _ssm_bu(ub, bmat_ref, j)

    for c in range(N_STATE // lc):
        cols = slice(c * lc, (c + 1) * lc)

        def slab(i, carry, cols=cols):
            pr, pi = carry
            r0 = pl.multiple_of(i * SUBLANES, SUBLANES)
            xr = bre[pl.ds(r0, SUBLANES), cols]
            xi = bim[pl.ds(r0, SUBLANES), cols]
            for s, k in enumerate((1, 2, 4)):
                mr = tab_ref[s, :, cols]
                mi = tab_ref[3 + s, :, cols]
                sr = pltpu.roll(xr, k, 0)
                si = pltpu.roll(xi, k, 0)
                xr, xi = xr + (mr * sr - mi * si), xi + (mr * si + mi * sr)
            qr = tab_ref[6, :, cols]
            qi = tab_ref[7, :, cols]
            xr, xi = xr + (qr * pr - qi * pi), xi + (qr * pi + qi * pr)
            bre[pl.ds(r0, SUBLANES), cols] = xr
            bim[pl.ds(r0, SUBLANES), cols] = xi
            return (jnp.broadcast_to(xr[SUBLANES - 1:SUBLANES, :], xr.shape),
                    jnp.broadcast_to(xi[SUBLANES - 1:SUBLANES, :], xi.shape))

        lr, li = lax.fori_loop(0, ts // SUBLANES, slab, (cre[:, cols], cim[:, cols]))
        cre[:, cols] = lr
        cim[:, cols] = li

    y_ref[0] = _ssm_out(bre[...], bim[...], u, cmat_ref, d_ref, wglu_ref, bglu_ref).astype(y_ref.dtype)
    hre_ref[0] = cre[...]
    him_ref[0] = cim[...]


def _ssm_prompt(u, bmat, cmat, tab, d_row, wglu, bglu, ts, lc=512, job=None):
    bsz, seq, _ = u.shape
    out_shape = (
        jax.ShapeDtypeStruct((bsz, seq, D_SSM), BF16),
        jax.ShapeDtypeStruct((bsz, SUBLANES, N_STATE), F32),
        jax.ShapeDtypeStruct((bsz, SUBLANES, N_STATE), F32),
    )
    in_specs = [
        pl.BlockSpec((1, ts, D_SSM), lambda b, l, *_: (b, l, 0)),
        _const_spec(bmat.shape), _const_spec(cmat.shape), _const_spec(tab.shape),
        _const_spec(d_row.shape), _const_spec(wglu.shape), _const_spec(bglu.shape),
    ]
    out_specs = (
        pl.BlockSpec((1, ts, D_SSM), lambda b, l, *_: (b, l, 0)),
        pl.BlockSpec((1, SUBLANES, N_STATE), lambda b, l, *_: (b, 0, 0)),
        pl.BlockSpec((1, SUBLANES, N_STATE), lambda b, l, *_: (b, 0, 0)),
    )
    return _tile_call(
        functools.partial(_ssm_prompt_kernel, ts=ts, lc=lc),
        grid=(bsz, seq // ts), in_specs=in_specs, out_specs=out_specs, out_shape=out_shape,
        scratch_shapes=[pltpu.VMEM((ts, N_STATE), F32), pltpu.VMEM((ts, N_STATE), F32),
                        pltpu.VMEM((SUBLANES, N_STATE), F32), pltpu.VMEM((SUBLANES, N_STATE), F32)],
        args=(u, bmat, cmat, tab, d_row, wglu, bglu), name="ssm_prompt", job=job)


def _ssm_sample_kernel(u_ref, h0r_ref, h0i_ref, ar_ref, ai_ref, bmat_ref, cmat_ref, d_ref, wglu_ref,
                       bglu_ref, y_ref, hr_ref, hi_ref):
    u = u_ref[...]
    ub = u.astype(BF16)
    parts = [_ssm_bu(ub, bmat_ref, j) for j in range(SSM_CHUNKS)]
    ar = ar_ref[...]
    ai = ai_ref[...]
    h0r = h0r_ref[...]
    h0i = h0i_ref[...]
    hr = jnp.concatenate([p[0] for p in parts], axis=1) + (ar * h0r - ai * h0i)
    hi = jnp.concatenate([p[1] for p in parts], axis=1) + (ar * h0i + ai * h0r)
    hr_ref[...] = hr
    hi_ref[...] = hi
    y_ref[...] = _ssm_out(hr, hi, u, cmat_ref, d_ref, wglu_ref, bglu_ref).astype(y_ref.dtype)


def _ssm_sample(u, h0r, h0i, a_re, a_im, bmat, cmat, d_row, wglu, bglu):
    n = u.shape[0]
    args = (u, h0r, h0i, a_re, a_im, bmat, cmat, d_row, wglu, bglu)
    return pl.pallas_call(
        _ssm_sample_kernel,
        grid=(1,), in_specs=[_const_spec(a.shape) for a in args],
        out_specs=(_const_spec((n, D_SSM)), _const_spec((n, N_STATE)), _const_spec((n, N_STATE))),
        out_shape=(jax.ShapeDtypeStruct((n, D_SSM), BF16), jax.ShapeDtypeStruct((n, N_STATE), F32),
                   jax.ShapeDtypeStruct((n, N_STATE), F32)),
        compiler_params=_params(("arbitrary",)),
        name="ssm_sample",
    )(*args)


def _attn_prompt_kernel(qaug_ref, kaug_ref, vt_ref, o_ref, s_ref, *, tq, heads):
    qi = pl.program_id(2)

    def qk(kb, slot, hh):
        k0 = pl.multiple_of(kb * tq, tq)
        s_ref[slot, hh] = _dot(kaug_ref[0, hh, pl.ds(k0, tq), :], qaug_ref[0, hh])

    def softmax_pv(kb, slot, hh, state, masked):
        m, acc = state
        k0 = pl.multiple_of(kb * tq, tq)
        s = s_ref[slot, hh]
        if masked:
            krow = lax.broadcasted_iota(jnp.int32, (tq, tq), 0)
            qcol = lax.broadcasted_iota(jnp.int32, (tq, tq), 1)
            s = jnp.where(krow <= qcol, s, NEG_BIG)
        m_new = jnp.maximum(m, jnp.max(s, axis=0, keepdims=True))
        p = jnp.exp2(s - m_new).astype(BF16)
        pv = _dot(vt_ref[0, hh, :, pl.ds(k0, tq)], p)
        return m_new, jnp.exp2(m - m_new) * acc + pv

    def step(kb, cur, states):
        out = []
        qk(kb + 1, 1 - cur, 0)
        for hh in range(heads):
            if hh + 1 < heads:
                qk(kb + 1, 1 - cur, hh + 1)
            out.append(softmax_pv(kb, cur, hh, states[hh], False))
        return tuple(out)

    def finish(cur, states):
        outs = []
        for hh in range(heads):
            _, acc = softmax_pv(qi, cur, hh, states[hh], True)
            outs.append((acc[0:HEAD_DIM] / acc[HEAD_DIM:HEAD_DIM + 1]).T)
        o_ref[0] = jnp.concatenate(outs, axis=1).astype(o_ref.dtype)

    for hh in range(heads):
        qk(0, 0, hh)
    init = tuple((jnp.full((1, tq), NEG_BIG, F32), jnp.zeros((VAUG, tq), F32)) for _ in range(heads))
    states = lax.fori_loop(0, qi // 2, lambda i, st: step(2 * i + 1, 1, step(2 * i, 0, st)), init)
    odd = lax.rem(qi, 2) == 1

    @pl.when(odd)
    def _():
        finish(1, step(qi - 1, 0, states))

    @pl.when(jnp.logical_not(odd))
    def _():
        finish(0, states)


def _attn_prompt(qaug, kaug, vtb, tq, heads):
    bsz, _, _, seq = qaug.shape
    width = heads * HEAD_DIM
    in_specs = [
        pl.BlockSpec((1, heads, KAUG, tq), lambda b, h, q: (b, h, 0, q)),
        pl.BlockSpec((1, heads, seq, KAUG), lambda b, h, q: (b, h, 0, 0)),
        pl.BlockSpec((1, heads, VAUG, seq), lambda b, h, q: (b, h, 0, 0)),
    ]
    return pl.pallas_call(
        functools.partial(_attn_prompt_kernel, tq=tq, heads=heads),
        grid=(bsz, N_HEADS // heads, seq // tq),
        in_specs=in_specs,
        out_specs=pl.BlockSpec((1, tq, width), lambda b, h, q: (b, q, h)),
        out_shape=jax.ShapeDtypeStruct((bsz, seq, D_ATT), BF16),
        scratch_shapes=[pltpu.VMEM((2, heads, tq, tq), F32)],
        compiler_params=_params(("arbitrary", "arbitrary", "arbitrary")),
        name="attn_prompt",
    )(qaug, kaug, vtb)


def _mix_kernel(x_ref, ys_ref, ya_ref, wgs_ref, wga_ref, wbs_ref, wba_ref, wo_ref, g_ref, b_ref, o_ref, *,
                overlap):
    overlap()
    x = x_ref[0]
    xb = x.astype(BF16)
    gs = jax.nn.sigmoid(_dot(xb, wgs_ref[...]))
    ga = jax.nn.sigmoid(_dot(xb, wga_ref[...]))
    mixed = gs * _dot(ys_ref[0], wbs_ref[...]) + ga * _dot(ya_ref[0], wba_ref[...])
    t = ALPHA * x + _dot(mixed.astype(BF16), wo_ref[...])
    o_ref[0] = _layer_norm(t, g_ref[...], b_ref[...])


def _mix(x, ys, ya, wgs, wga, wbs, wba, wo, g, b, tm, job=None):
    bsz, seq, _ = x.shape
    ws = (wgs, wga, wbs, wba, wo, g, b)
    in_specs = [
        pl.BlockSpec((1, tm, D_MODEL), lambda i, l, *_: (i, l, 0)),
        pl.BlockSpec((1, tm, D_SSM), lambda i, l, *_: (i, l, 0)),
        pl.BlockSpec((1, tm, D_ATT), lambda i, l, *_: (i, l, 0)),
    ] + [_const_spec(w.shape) for w in ws]
    return _tile_call(
        _mix_kernel, grid=(bsz, seq // tm), in_specs=in_specs,
        out_specs=(pl.BlockSpec((1, tm, D_MODEL), lambda i, l, *_: (i, l, 0)),),
        out_shape=(jax.ShapeDtypeStruct((bsz, seq, D_MODEL), F32),),
        scratch_shapes=[], args=(x, ys, ya, *ws), name="mix_ln1", job=job)


def _ffn_tail(x, a, am1, am2, bgate, cw_ref, cb_ref, wdn_ref, g_ref, b_ref, wpg_ref, bpg_ref, wple_ref, p):
    c = cb_ref[...] + cw_ref[0:1, :] * am2 + cw_ref[1:2, :] * am1 + cw_ref[2:3, :] * a
    hcat = (jax.nn.gelu(c) * bgate).astype(BF16)
    x2 = _layer_norm(ALPHA * x + _dot(hcat, wdn_ref[...]), g_ref[...], b_ref[...])
    gate = jax.nn.sigmoid(_dot(x2.astype(BF16), wpg_ref[...]) + bpg_ref[...])
    return x2 + gate * _dot(p.astype(BF16), wple_ref[...])


def _ffn_prompt_kernel(x_ref, p_ref, wup_ref, cw_ref, cb_ref, wdn_ref, g_ref, b_ref, wpg_ref, bpg_ref,
                       wple_ref, o_ref, tail_ref, prev_ref, *, tm, overlap):
    @pl.when(pl.program_id(1) == 0)
    def _():
        prev_ref[...] = jnp.zeros_like(prev_ref)

    overlap()
    x = x_ref[0]
    up = _dot(x.astype(BF16), wup_ref[...])
    a = up[:, :D_FF]
    row = lax.broadcasted_iota(jnp.int32, (tm, D_FF), 0)
    p1 = jnp.broadcast_to(prev_ref[SUBLANES - 1:SUBLANES, :], (tm, D_FF))
    p2 = jnp.broadcast_to(prev_ref[SUBLANES - 2:SUBLANES - 1, :], (tm, D_FF))
    am1 = jnp.where(row == 0, p1, pltpu.roll(a, 1, 0))
    am2 = jnp.where(row == 0, p2, jnp.where(row == 1, p1, pltpu.roll(a, 2, 0)))
    tail = a[tm - SUBLANES:, :]
    prev_ref[...] = tail
    tail_ref[0] = tail
    o_ref[0] = _ffn_tail(x, a, am1, am2, up[:, D_FF:], cw_ref, cb_ref, wdn_ref, g_ref, b_ref,
                         wpg_ref, bpg_ref, wple_ref, p_ref[0])


def _ffn_sample_kernel(x_ref, p_ref, s0_ref, s1_ref, wup_ref, cw_ref, cb_ref, wdn_ref, g_ref, b_ref,
                       wpg_ref, bpg_ref, wple_ref, o_ref, a_ref):
    x = x_ref[...]
    up = _dot(x.astype(BF16), wup_ref[...])
    a = up[:, :D_FF]
    a_ref[...] = a
    o_ref[...] = _ffn_tail(x, a, s1_ref[...], s0_ref[...], up[:, D_FF:], cw_ref, cb_ref, wdn_ref, g_ref,
                           b_ref, wpg_ref, bpg_ref, wple_ref, p_ref[...])


def _ffn_prompt(x, p, ws, tm, job=None):
    bsz, seq, _ = x.shape
    in_specs = [
        pl.BlockSpec((1, tm, D_MODEL), lambda i, l, *_: (i, l, 0)),
        pl.BlockSpec((1, tm, p.shape[-1]), lambda i, l, *_: (i, l, 0)),
    ] + [_resident_spec(w.shape) for w in ws]
    return _tile_call(
        functools.partial(_ffn_prompt_kernel, tm=tm), grid=(bsz, seq // tm), in_specs=in_specs,
        out_specs=(pl.BlockSpec((1, tm, D_MODEL), lambda i, l, *_: (i, l, 0)),
                   pl.BlockSpec((1, SUBLANES, D_FF), lambda i, l, *_: (i, 0, 0))),
        out_shape=(jax.ShapeDtypeStruct((bsz, seq, D_MODEL), F32),
                   jax.ShapeDtypeStruct((bsz, SUBLANES, D_FF), F32)),
        scratch_shapes=[pltpu.VMEM((SUBLANES, D_FF), F32)],
        args=(x, p, *ws), name="ffn_prompt", job=job)


def _ffn_sample(x, p, s0, s1, ws):
    n = x.shape[0]
    args = (x, p, s0, s1) + tuple(ws)
    return pl.pallas_call(
        _ffn_sample_kernel, grid=(1,), in_specs=[_const_spec(a.shape) for a in args],
        out_specs=(_const_spec((n, D_MODEL)), _const_spec((n, D_FF))),
        out_shape=(jax.ShapeDtypeStruct((n, D_MODEL), F32), jax.ShapeDtypeStruct((n, D_FF), F32)),
        compiler_params=_params(("arbitrary",)),
        name="ffn_sample",
    )(*args)


def _inproj_sample_kernel(x_ref, wu_ref, wt_ref, bfc_ref, u_ref, qt_ref, kt_ref, vt_ref, lft_ref):
    xb = x_ref[...].astype(BF16)
    u_ref[...] = _dot(xb, wu_ref[...])
    zt = _dot_nt(wt_ref[...], xb)
    qt_ref[...] = zt[0:D_ATT, :]
    kt_ref[...] = zt[D_ATT:2 * D_ATT, :]
    vt_ref[...] = zt[2 * D_ATT:3 * D_ATT, :]
    lft_ref[...] = _log_sigmoid(zt[3 * D_ATT:3 * D_ATT + N_HEADS, :] + bfc_ref[:, 0:1])


def _inproj_sample(x, wu, wt, bf_col):
    n = x.shape[0]
    args = (x, wu, wt, bf_col)
    shp = lambda r: jax.ShapeDtypeStruct((r, n), F32)
    return pl.pallas_call(
        _inproj_sample_kernel, grid=(1,), in_specs=[_const_spec(a.shape) for a in args],
        out_specs=(_const_spec((n, D_SSM)), _const_spec((D_ATT, n)), _const_spec((D_ATT, n)),
                   _const_spec((D_ATT, n)), _const_spec((N_HEADS, n))),
        out_shape=(jax.ShapeDtypeStruct((n, D_SSM), F32), shp(D_ATT), shp(D_ATT), shp(D_ATT), shp(N_HEADS)),
        compiler_params=_params(("arbitrary",)),
        name="inproj_sample",
    )(*args)


def _decode_only(job):
    def body(*, overlap):
        overlap()

    return _tile_call(body, grid=(job.count, job.steps_per_seq), in_specs=[], out_specs=(), out_shape=(),
                      scratch_shapes=[], args=(), name="decode_attn", job=job)[0]


def _discretize(a_re, a_im, log_dt, b_re, b_im):
    dt = jnp.exp(log_dt)[:, None]
    mag = jnp.exp(dt * a_re)
    ang = dt * a_im
    e_re = mag * jnp.cos(ang)
    e_im = mag * jnp.sin(ang)
    nr = e_re - 1.0
    ni = e_im
    den = a_re * a_re + a_im * a_im
    z_re = ((nr * a_re + ni * a_im) / den)[..., None]
    z_im = ((ni * a_re - nr * a_im) / den)[..., None]
    return e_re, e_im, z_re * b_re - z_im * b_im, z_re * b_im + z_im * b_re


def _block_diag(blocks):
    g, r, c = blocks.shape
    eye = jnp.eye(g, dtype=blocks.dtype)
    return (eye[:, None, :, None] * blocks[:, :, None, :]).reshape(g * r, g * c)


def _cmul(ar, ai, br, bi):
    return ar * br - ai * bi, ar * bi + ai * br


def _scan_tables(e_re, e_im):
    a1r, a1i = e_re.reshape(1, N_STATE), e_im.reshape(1, N_STATE)
    a2r, a2i = _cmul(a1r, a1i, a1r, a1i)
    a4r, a4i = _cmul(a2r, a2i, a2r, a2i)
    rows = jnp.arange(SUBLANES)[:, None]
    tabs_r, tabs_i = [], []
    for k, (pr, pi) in zip((1, 2, 4), ((a1r, a1i), (a2r, a2i), (a4r, a4i))):
        tabs_r.append(jnp.where(rows >= k, pr, 0.0))
        tabs_i.append(jnp.where(rows >= k, pi, 0.0))
    pw_r, pw_i = [a1r], [a1i]
    for _ in range(SUBLANES - 1):
        nr, ni = _cmul(pw_r[-1], pw_i[-1], a1r, a1i)
        pw_r.append(nr)
        pw_i.append(ni)
    return jnp.stack(tabs_r + tabs_i + [jnp.concatenate(pw_r, 0), jnp.concatenate(pw_i, 0)])


def kernel(x_prompt, x_sample, p_prompt, p_sample, cache_k, cache_v, cache_logf, page_table, state_ssm_re, state_ssm_im, state_conv, w_in, b_f, ssm_a_re, ssm_a_im, ssm_log_dt, ssm_b_re, ssm_b_im, ssm_c_re, ssm_c_im, ssm_d, w_glu, b_glu, w_br_ssm, w_br_att, w_o, ln1_g, ln1_b, w_up, conv_w, conv_b, w_down, ln2_g, ln2_b, w_ple_gate, b_ple_gate, w_ple):
    assert w_in.shape[0] == DEPTH
    bsz, seq, _ = x_prompt.shape
    nb = x_sample.shape[0]
    tm = 256

    win = w_in[0]
    o = [0, D_SSM, D_SSM + D_ATT, D_SSM + 2 * D_ATT, D_SSM + 3 * D_ATT, D_SSM + 3 * D_ATT + N_HEADS]
    wu, wq, wk, wv, wf = (win[:, o[i]:o[i + 1]] for i in range(5))
    wgs = win[:, o[5]:o[5] + D_MODEL].astype(BF16)
    wga = win[:, o[5] + D_MODEL:].astype(BF16)
    scale = HEAD_DIM ** -0.5
    wk_pad = jnp.pad(wk.reshape(D_MODEL, N_HEADS, HEAD_DIM), ((0, 0), (0, 0), (0, KAUG - HEAD_DIM)))
    wf_pad = jnp.pad(wf, ((0, 0), (0, LANES - N_HEADS)))
    w1 = jnp.concatenate([wu, wk_pad.reshape(D_MODEL, N_HEADS * KAUG), wf_pad], axis=1).astype(BF16)
    wt = jnp.concatenate([wq * scale, wk, wv, wf], axis=1).T.astype(BF16)
    wu_b = wu.astype(BF16)
    bf_row = jnp.pad(b_f, ((0, 0), (0, LANES - N_HEADS)))
    bf_col = jnp.broadcast_to(b_f[0][:, None], (N_HEADS, LANES))
    tri = jnp.tri(tm, dtype=BF16)
    pidx = jnp.arange(3 * LANES)
    place = jnp.zeros((3 * LANES, N_HEADS * KAUG), BF16).at[
        pidx, (pidx % LANES) * KAUG + HEAD_DIM + pidx // LANES].set(
        jnp.where(pidx % LANES < N_HEADS, 1.0, 0.0).astype(BF16), mode="drop")

    e_re, e_im, bb_re, bb_im = _discretize(ssm_a_re[0], ssm_a_im[0], ssm_log_dt[0], ssm_b_re[0], ssm_b_im[0])
    gpc = N_GROUPS // SSM_CHUNKS
    chunks = [slice(j * gpc, (j + 1) * gpc) for j in range(SSM_CHUNKS)]
    bt_re, bt_im = jnp.swapaxes(bb_re, 1, 2), jnp.swapaxes(bb_im, 1, 2)
    ct_re, ct_im = jnp.swapaxes(ssm_c_re[0], 1, 2), jnp.swapaxes(-ssm_c_im[0], 1, 2)
    bmat = jnp.stack([jnp.concatenate([_block_diag(bt_re[c]), _block_diag(bt_im[c])], axis=1)
                      for c in chunks]).astype(BF16)
    cmat = jnp.stack([jnp.concatenate([_block_diag(ct_re[c]), _block_diag(ct_im[c])], axis=0)
                      for c in chunks]).astype(BF16)
    tab = _scan_tables(e_re, e_im)
    a_re_row, a_im_row = e_re.reshape(1, N_STATE), e_im.reshape(1, N_STATE)
    wglu = w_glu[0].astype(BF16)
    mix_ws = (wgs, wga, w_br_ssm[0].astype(BF16), w_br_att[0].astype(BF16), w_o[0].astype(BF16), ln1_g, ln1_b)
    ffn_ws = (w_up[0].astype(BF16), conv_w[0], conv_b, w_down[0].astype(BF16), ln2_g, ln2_b,
              w_ple_gate[0].astype(BF16), b_ple_gate, w_ple[0].astype(BF16))

    xs = x_sample[:, 0]
    us, qts, kts, vts, lfts = _inproj_sample(xs, wu_b, wt, bf_col)
    kc = jnp.transpose(cache_k[0], (0, 2, 3, 1))
    vc = jnp.transpose(cache_v[0], (0, 2, 3, 1))
    lfc = jnp.transpose(cache_logf[0], (0, 2, 1))
    shp3 = (N_HEADS, HEAD_DIM, nb)
    todo = _DecodeJob(page_table.reshape(-1), kc, vc, lfc, qts.reshape(shp3), kts.reshape(shp3),
                      vts.reshape(shp3), lfts, 0, nb)
    host_steps = bsz * (seq // tm)
    decoded = []

    def hosted(call, *args, **kwargs):
        nonlocal todo
        count = host_steps // todo.steps_per_seq
        if count * todo.steps_per_seq != host_steps or count > todo.count:
            return call(*args, **kwargs)
        *outs, dec = call(*args, **kwargs, job=todo._replace(count=count))
        decoded.append(dec)
        todo = todo._replace(first=todo.first + count, count=todo.count - count)
        return outs

    u, kaug, qaug, kt, vt, vtb, lft = hosted(_inproj_prompt, x_prompt, w1, wt, bf_row, bf_col, tri, place, tm)
    ys, hre, him = hosted(_ssm_prompt, u, bmat, cmat, tab, ssm_d, wglu, b_glu, ts=tm)
    ya = _attn_prompt(qaug, kaug, vtb, tq=512, heads=4)
    x1, = hosted(_mix, x_prompt, ys, ya, *mix_ws, tm=tm)
    y_prompt, tail = hosted(_ffn_prompt, x1, p_prompt[0], ffn_ws, tm=tm)
    if todo.count:
        decoded.append(_decode_only(todo))

    k_prompt = jnp.transpose(kt.reshape(bsz, N_HEADS, HEAD_DIM, seq), (0, 3, 1, 2))[None]
    v_prompt = jnp.transpose(vt.reshape(bsz, N_HEADS, HEAD_DIM, seq), (0, 3, 1, 2))[None]
    logf_prompt = jnp.transpose(lft, (0, 2, 1))[None]
    ssm_re_prompt = hre[:, 0].reshape(1, bsz, N_GROUPS, STATE_DIM)
    ssm_im_prompt = him[:, 0].reshape(1, bsz, N_GROUPS, STATE_DIM)
    conv_prompt = tail[:, SUBLANES - (CONV_W - 1):][None]

    ys_s, hr_s, hi_s = _ssm_sample(us, state_ssm_re[0].reshape(nb, N_STATE), state_ssm_im[0].reshape(nb, N_STATE),
                                   a_re_row, a_im_row, bmat, cmat, ssm_d, wglu, b_glu)
    yat = functools.reduce(lambda x, y: x + y, decoded)
    ya_s = yat.reshape(D_ATT, nb).T.astype(BF16)
    x1_s = _mix(xs[None], ys_s[None], ya_s[None], *mix_ws, tm=nb)[0][0]
    y_s, a_s = _ffn_sample(x1_s, p_sample[0, :, 0], state_conv[0, :, 0], state_conv[0, :, 1], ffn_ws)

    y_sample = y_s[:, None, :]
    k_sample = kts.T.reshape(1, nb, 1, N_HEADS, HEAD_DIM)
    v_sample = vts.T.reshape(1, nb, 1, N_HEADS, HEAD_DIM)
    logf_sample = lfts.T.reshape(1, nb, 1, N_HEADS)
    ssm_re_sample = hr_s.reshape(1, nb, N_GROUPS, STATE_DIM)
    ssm_im_sample = hi_s.reshape(1, nb, N_GROUPS, STATE_DIM)
    conv_sample = jnp.stack([state_conv[0, :, 1], a_s], axis=1)[None]

    return (y_prompt, y_sample, k_prompt, v_prompt, logf_prompt, ssm_re_prompt, ssm_im_prompt, conv_prompt,
            k_sample, v_sample, logf_sample, ssm_re_sample, ssm_im_sample, conv_sample)
```

```python
import functools
from typing import NamedTuple

import jax
import jax.numpy as jnp
from jax import lax
from jax.experimental import pallas as pl
from jax.experimental.pallas import tpu as pltpu

F32 = jnp.float32
BF16 = jnp.bfloat16

D_MODEL = 1024
D_SSM = 512
SSM_GROUP = 16
N_GROUPS = 32
STATE_DIM = 64
N_STATE = N_GROUPS * STATE_DIM
N_HEADS = 8
HEAD_DIM = 64
D_ATT = 512
D_FF = 2816
CONV_W = 3
DEPTH = 1
ALPHA = (2 * DEPTH) ** 0.25
LN_EPS = 1e-5
NEG_BIG = -1e30
LANES = 128
SUBLANES = 8
KAUG = 128
VAUG = 80
LOG2E = 1.4426950408889634
VMEM_LIMIT = 56 * 1024 * 1024


def _dot(a, b):
    return jnp.dot(a, b, preferred_element_type=F32)


def _dot_nt(a, b):
    return lax.dot_general(a, b, (((1,), (1,)), ((), ())), preferred_element_type=F32)


def _log_sigmoid(x):
    return jnp.minimum(x, 0.0) - jnp.log1p(jnp.exp(-jnp.abs(x)))


def _split3(x):
    hi = x.astype(BF16)
    r1 = x - hi.astype(F32)
    mid = r1.astype(BF16)
    lo = (r1 - mid.astype(F32)).astype(BF16)
    return hi, mid, lo


def _layer_norm(t, g, b):
    mu = jnp.mean(t, axis=-1, keepdims=True)
    c = t - mu
    var = jnp.mean(c * c, axis=-1, keepdims=True)
    return c * lax.rsqrt(var + LN_EPS) * g + b


def _params(sem):
    return pltpu.CompilerParams(dimension_semantics=sem, vmem_limit_bytes=VMEM_LIMIT)


def _const_spec(shape):
    nd = len(shape)
    return pl.BlockSpec(shape, lambda *_: (0,) * nd)


def _resident_spec(shape):
    nd = len(shape)
    return pl.BlockSpec(shape, lambda *_: (0,) * nd, pipeline_mode=pl.Buffered(1))


DEC_PAGES = 16


class _DecodeJob(NamedTuple):
    page_flat: jax.Array
    kc: jax.Array
    vc: jax.Array
    lfc: jax.Array
    qt: jax.Array
    ktn: jax.Array
    vtn: jax.Array
    lfn: jax.Array
    first: int
    count: int

    @property
    def pages_per_seq(self):
        return self.page_flat.shape[0] // self.qt.shape[-1]

    @property
    def steps_per_seq(self):
        return self.pages_per_seq // DEC_PAGES


class _DecodeRefs(NamedTuple):
    k: tuple
    v: tuple
    lf: tuple
    qt: object
    ktn: object
    vtn: object
    lfn: object
    out: object
    qb: object
    m: object
    l: object
    acc: object
    carry: object


def _dec_column(x, e):
    lane = lax.broadcasted_iota(jnp.int32, x.shape, 1)
    col = jnp.sum(jnp.where(lane == e, x, 0.0), axis=1, keepdims=True)
    return jnp.broadcast_to(col, (x.shape[0], LANES))


def _dec_begin(d, e, jj, first_step):
    @pl.when(first_step)
    def _():
        d.out[...] = jnp.zeros_like(d.out)

    @pl.when(jj == 0)
    def _():
        for h in range(N_HEADS):
            d.qb[h] = _dec_column(d.qt[h], e)
        d.m[...] = jnp.full_like(d.m, NEG_BIG)
        d.l[...] = jnp.zeros_like(d.l)
        d.acc[...] = jnp.zeros_like(d.acc)
        d.carry[...] = _dec_column(d.lfn[...], e)


def _dec_main(d):
    pg = len(d.k)
    sub = lax.broadcasted_iota(jnp.int32, (N_HEADS, LANES), 0)
    lf_all = jnp.concatenate([d.lf[i][0] for i in range(pg)], axis=0)
    row = lax.broadcasted_iota(jnp.int32, (LANES, 2 * LANES), 0)
    col = lax.broadcasted_iota(jnp.int32, (LANES, 2 * LANES), 1)
    later = jnp.where(jnp.logical_or(row > col, col >= LANES), 1.0, 0.0).astype(BF16)
    sums = _dot(jnp.concatenate(_split3(lf_all), axis=0), later)
    n = pg * N_HEADS
    sums = (sums[0:n] + sums[n:2 * n]) + sums[2 * n:3 * n]
    carry = d.carry[...]
    logits = [None] * pg
    for i in reversed(range(pg)):
        logits[i] = carry + sums[i * N_HEADS:(i + 1) * N_HEADS, 0:LANES]
        carry = carry + sums[i * N_HEADS:(i + 1) * N_HEADS, LANES:2 * LANES]
    d.carry[...] = carry

    for h in range(N_HEADS):
        qh = d.qb[h]
        for i in range(pg):
            r = jnp.sum(d.k[i][0, h] * qh, axis=0, keepdims=True)
            logits[i] = logits[i] + jnp.where(sub == h, jnp.broadcast_to(r, logits[i].shape), 0.0)
    m_old = d.m[...]
    m_new = m_old
    for s in logits:
        m_new = jnp.maximum(m_new, s)
    alpha = jnp.exp(m_old - m_new)
    probs = [jnp.exp(s - m_new) for s in logits]
    d.l[...] = alpha * d.l[...] + functools.reduce(lambda x, y: x + y, probs)
    d.m[...] = m_new
    for h in range(N_HEADS):
        acc = d.acc[h] * jnp.broadcast_to(alpha[h:h + 1, :], (HEAD_DIM, LANES))
        for i in range(pg):
            acc = acc + d.v[i][0, h] * jnp.broadcast_to(probs[i][h:h + 1, :], (HEAD_DIM, LANES))
        d.acc[h] = acc


def _dec_end(d, e, jj, steps_per_seq):
    @pl.when(jj == steps_per_seq - 1)
    def _():
        sub = lax.broadcasted_iota(jnp.int32, (N_HEADS, LANES), 0)
        m = d.m[...]
        l = d.l[...]
        s_new = jnp.zeros((N_HEADS, LANES), F32)
        for h in range(N_HEADS):
            r = jnp.sum(_dec_column(d.ktn[h], e) * d.qb[h], axis=0, keepdims=True)
            s_new = s_new + jnp.where(sub == h, jnp.broadcast_to(r, s_new.shape), 0.0)
        m_tot = jnp.maximum(jnp.broadcast_to(jnp.max(m, axis=1, keepdims=True), m.shape), s_new)
        w = jnp.exp(m - m_tot)
        p_new = jnp.exp(s_new - m_tot)
        l_tot = jnp.broadcast_to(jnp.sum(l * w, axis=1, keepdims=True), l.shape) + p_new
        inv = 1.0 / l_tot
        olane = lax.broadcasted_iota(jnp.int32, (HEAD_DIM, d.out.shape[-1]), 1)
        for h in range(N_HEADS):
            wh = jnp.broadcast_to(w[h:h + 1, :], (HEAD_DIM, LANES))
            num = jnp.sum(d.acc[h] * wh, axis=1, keepdims=True)
            num = jnp.broadcast_to(num, (HEAD_DIM, LANES)) + _dec_column(d.vtn[h], e) * jnp.broadcast_to(
                p_new[h:h + 1, :], (HEAD_DIM, LANES))
            val = num * jnp.broadcast_to(inv[h:h + 1, :], (HEAD_DIM, LANES))
            d.out[h] = jnp.where(olane == e, val, d.out[h])


def _tile_call(body, *, grid, in_specs, out_specs, out_shape, scratch_shapes, args, name, job=None):
    sem = ("arbitrary",) * len(grid)
    if job is None:
        return pl.pallas_call(
            functools.partial(body, overlap=lambda: None), grid=grid, in_specs=in_specs, out_specs=out_specs,
            out_shape=out_shape, scratch_shapes=scratch_shapes, compiler_params=_params(sem), name=name)(*args)

    nb = job.qt.shape[-1]
    page = job.kc.shape[-1]
    assert page == LANES and nb == LANES
    pps, sps = job.pages_per_seq, job.steps_per_seq
    n_steps = 1
    for g in grid:
        n_steps *= g
    assert n_steps == job.count * sps, (n_steps, job.count, sps)

    def locate(ids):
        step = ids[0]
        for g, i in zip(grid[1:], ids[1:]):
            step = step * g + i
        return job.first + lax.div(step, jnp.int32(sps)), lax.rem(step, jnp.int32(sps)), step == 0

    def page_index(i, trailing):
        def index_map(*a):
            e, jj, _ = locate(a[:-1])
            return (a[-1][e * pps + (sps - 1 - jj) * DEC_PAGES + i],) + (0,) * trailing
        return index_map

    dec_in_specs = ([pl.BlockSpec((1, N_HEADS, HEAD_DIM, page), page_index(i, 3)) for i in range(DEC_PAGES)] * 2
                    + [pl.BlockSpec((1, N_HEADS, page), page_index(i, 2)) for i in range(DEC_PAGES)]
                    + [_const_spec((N_HEADS, HEAD_DIM, nb))] * 3 + [_const_spec((N_HEADS, nb))])
    dec_args = ([job.kc] * DEC_PAGES + [job.vc] * DEC_PAGES + [job.lfc] * DEC_PAGES
                + [job.qt, job.ktn, job.vtn, job.lfn])
    dec_scratch = [pltpu.VMEM((N_HEADS, HEAD_DIM, LANES), F32), pltpu.VMEM((N_HEADS, LANES), F32),
                   pltpu.VMEM((N_HEADS, LANES), F32), pltpu.VMEM((N_HEADS, HEAD_DIM, LANES), F32),
                   pltpu.VMEM((N_HEADS, LANES), F32)]
    n_in, n_out, n_scr = len(in_specs), len(out_specs), len(scratch_shapes)

    def fused(pt_ref, *refs):
        del pt_ref
        host_in, dec_in = refs[:n_in], refs[n_in:n_in + len(dec_in_specs)]
        rest = refs[n_in + len(dec_in_specs):]
        host_out, dec_out, scr = rest[:n_out], rest[n_out], rest[n_out + 1:]
        d = _DecodeRefs(dec_in[:DEC_PAGES], dec_in[DEC_PAGES:2 * DEC_PAGES], dec_in[2 * DEC_PAGES:3 * DEC_PAGES],
                        *dec_in[3 * DEC_PAGES:], dec_out, *scr[n_scr:])
        e, jj, first_step = locate([pl.program_id(a) for a in range(len(grid))])

        def overlap():
            _dec_begin(d, e, jj, first_step)
            _dec_main(d)

        body(*host_in, *host_out, *scr[:n_scr], overlap=overlap)
        _dec_end(d, e, jj, sps)

    grid_spec = pltpu.PrefetchScalarGridSpec(
        num_scalar_prefetch=1, grid=grid, in_specs=list(in_specs) + dec_in_specs,
        out_specs=tuple(out_specs) + (_const_spec((N_HEADS, HEAD_DIM, nb)),),
        scratch_shapes=list(scratch_shapes) + dec_scratch)
    return pl.pallas_call(
        fused, grid_spec=grid_spec,
        out_shape=tuple(out_shape) + (jax.ShapeDtypeStruct((N_HEADS, HEAD_DIM, nb), F32),),
        compiler_params=_params(sem), name=name)(job.page_flat, *args, *dec_args)


def _inproj_prompt_kernel(x_ref, w1_ref, wt_ref, bfr_ref, bfc_ref, tri_ref, place_ref,
                          u_ref, kaug_ref, qaug_ref, kt_ref, vt_ref, vtb_ref, lft_ref, carry_ref, *, tm, overlap):
    @pl.when(pl.program_id(1) == 0)
    def _():
        carry_ref[...] = jnp.zeros_like(carry_ref)

    overlap()
    xb = x_ref[0].astype(BF16)
    z1 = _dot(xb, w1_ref[...])
    u_ref[0] = z1[:, :D_SSM]
    kpad = z1[:, D_SSM:D_SSM + N_HEADS * KAUG]
    f_logit = z1[:, D_SSM + N_HEADS * KAUG:]

    lane = lax.broadcasted_iota(jnp.int32, (tm, LANES), 1)
    lf = jnp.where(lane < N_HEADS, _log_sigmoid(f_logit + bfr_ref[...]), 0.0)
    hi, mid, lo = _split3(lf)
    cum = _dot(tri_ref[...], jnp.concatenate([hi, mid, lo], axis=1))
    fcum = (cum[:, :LANES] + cum[:, LANES:2 * LANES]) + cum[:, 2 * LANES:] + carry_ref[0:1, :]
    carry_ref[...] = jnp.broadcast_to(fcum[tm - 1:tm, :], carry_ref.shape)
    nhi, nmid, nlo = _split3(fcum * (-LOG2E))
    kaug = kpad + _dot(jnp.concatenate([nhi, nmid, nlo], axis=1), place_ref[...])
    kaug = kaug.astype(BF16)
    for h in range(N_HEADS):
        kaug_ref[0, h] = kaug[:, h * KAUG:(h + 1) * KAUG]

    zt = _dot_nt(wt_ref[...], xb)
    row = lax.broadcasted_iota(jnp.int32, (KAUG - HEAD_DIM, tm), 0)
    ones_rows = jnp.where(row < 3, 1.0, 0.0).astype(BF16)
    for h in range(N_HEADS):
        qaug_ref[0, h, 0:HEAD_DIM, :] = (zt[h * HEAD_DIM:(h + 1) * HEAD_DIM, :] * LOG2E).astype(BF16)
        qaug_ref[0, h, HEAD_DIM:KAUG, :] = ones_rows
    kt_ref[0] = zt[D_ATT:2 * D_ATT, :]
    vt = zt[2 * D_ATT:3 * D_ATT, :]
    vt_ref[0] = vt
    vrow = lax.broadcasted_iota(jnp.int32, (VAUG - HEAD_DIM, tm), 0)
    one_row = jnp.where(vrow == 0, 1.0, 0.0).astype(BF16)
    for h in range(N_HEADS):
        vtb_ref[0, h, 0:HEAD_DIM, :] = vt[h * HEAD_DIM:(h + 1) * HEAD_DIM, :].astype(BF16)
        vtb_ref[0, h, HEAD_DIM:VAUG, :] = one_row
    lft_ref[0] = _log_sigmoid(zt[3 * D_ATT:3 * D_ATT + N_HEADS, :] + bfc_ref[:, 0:1])


def _inproj_prompt(x, w1, wt, bf_row, bf_col, tri, place, tm, job=None):
    bsz, seq, _ = x.shape
    out_shape = (
        jax.ShapeDtypeStruct((bsz, seq, D_SSM), F32),
        jax.ShapeDtypeStruct((bsz, N_HEADS, seq, KAUG), BF16),
        jax.ShapeDtypeStruct((bsz, N_HEADS, KAUG, seq), BF16),
        jax.ShapeDtypeStruct((bsz, D_ATT, seq), F32),
        jax.ShapeDtypeStruct((bsz, D_ATT, seq), F32),
        jax.ShapeDtypeStruct((bsz, N_HEADS, VAUG, seq), BF16),
        jax.ShapeDtypeStruct((bsz, N_HEADS, seq), F32),
    )
    in_specs = [
        pl.BlockSpec((1, tm, D_MODEL), lambda b, l, *_: (b, l, 0)),
        _const_spec(w1.shape), _const_spec(wt.shape), _const_spec(bf_row.shape),
        _const_spec(bf_col.shape), _const_spec(tri.shape), _const_spec(place.shape),
    ]
    out_specs = (
        pl.BlockSpec((1, tm, D_SSM), lambda b, l, *_: (b, l, 0)),
        pl.BlockSpec((1, N_HEADS, tm, KAUG), lambda b, l, *_: (b, 0, l, 0)),
        pl.BlockSpec((1, N_HEADS, KAUG, tm), lambda b, l, *_: (b, 0, 0, l)),
        pl.BlockSpec((1, D_ATT, tm), lambda b, l, *_: (b, 0, l)),
        pl.BlockSpec((1, D_ATT, tm), lambda b, l, *_: (b, 0, l)),
        pl.BlockSpec((1, N_HEADS, VAUG, tm), lambda b, l, *_: (b, 0, 0, l)),
        pl.BlockSpec((1, N_HEADS, tm), lambda b, l, *_: (b, 0, l)),
    )
    return _tile_call(
        functools.partial(_inproj_prompt_kernel, tm=tm),
        grid=(bsz, seq // tm), in_specs=in_specs, out_specs=out_specs, out_shape=out_shape,
        scratch_shapes=[pltpu.VMEM((SUBLANES, LANES), F32)],
        args=(x, w1, wt, bf_row, bf_col, tri, place), name="inproj_prompt", job=job)


SSM_CHUNKS = 4
CH_IN = D_SSM // SSM_CHUNKS
CH_ST = N_STATE // SSM_CHUNKS


def _ssm_bu(ub, bmat_ref, j):
    part = _dot(ub[:, j * CH_IN:(j + 1) * CH_IN], bmat_ref[j])
    return part[:, :CH_ST], part[:, CH_ST:]


def _ssm_out(h_re, h_im, u, cmat_ref, d_ref, wglu_ref, bglu_ref):
    ys = []
    for j in range(SSM_CHUNKS):
        st = slice(j * CH_ST, (j + 1) * CH_ST)
        hb = jnp.concatenate([h_re[:, st].astype(BF16), h_im[:, st].astype(BF16)], axis=1)
        ys.append(_dot(hb, cmat_ref[j]))
    y = jnp.concatenate(ys, axis=1) + d_ref[...] * u
    y = jax.nn.gelu(y)
    return y * jax.nn.sigmoid(_dot(y.astype(BF16), wglu_ref[...]) + bglu_ref[...])


def _ssm_prompt_kernel(u_ref, bmat_ref, cmat_ref, tab_ref, d_ref, wglu_ref, bglu_ref,
                       y_ref, hre_ref, him_ref, bre, bim, cre, cim, *, ts, lc, overlap):
    @pl.when(pl.program_id(1) == 0)
    def _():
        cre[...] = jnp.zeros_like(cre)
        cim[...] = jnp.zeros_like(cim)

    overlap()
    u = u_ref[0]
    ub = u.astype(BF16)
    n_cols = N_STATE // lc
    cols = [slice(c * lc, (c + 1) * lc) for c in range(n_cols)]

    def slab(r0, carry, c):
        pr, pi = carry
        xr = bre[pl.ds(r0, SUBLANES), cols[c]]
        xi = bim[pl.ds(r0, SUBLANES), cols[c]]
        for s, k in enumerate((1, 2, 4)):
            mr = tab_ref[s, :, cols[c]]
            mi = tab_ref[3 + s, :, cols[c]]
            sr = pltpu.roll(xr, k, 0)
            si = pltpu.roll(xi, k, 0)
            xr, xi = xr + (mr * sr - mi * si), xi + (mr * si + mi * sr)
        qr = tab_ref[6, :, cols[c]]
        qi = tab_ref[7, :, cols[c]]
        xr, xi = xr + (qr * pr - qi * pi), xi + (qr * pi + qi * pr)
        bre[pl.ds(r0, SUBLANES), cols[c]] = xr
        bim[pl.ds(r0, SUBLANES), cols[c]] = xi
        return (jnp.broadcast_to(xr[SUBLANES - 1:SUBLANES, :], xr.shape),
                jnp.broadcast_to(xi[SUBLANES - 1:SUBLANES, :], xi.shape))

    for j in range(SSM_CHUNKS):
        st = slice(j * CH_ST, (j + 1) * CH_ST)
        bre[:, st], bim[:, st] = _ssm_bu(ub, bmat_ref, j)
    for c in range(n_cols):
        carry = (cre[:, cols[c]], cim[:, cols[c]])
        for i in range(ts // SUBLANES):
            carry = slab(i * SUBLANES, carry, c)
        cre[:, cols[c]], cim[:, cols[c]] = carry
    y_ref[0] = _ssm_out(bre[...], bim[...], u, cmat_ref, d_ref, wglu_ref, bglu_ref).astype(y_ref.dtype)
    hre_ref[0] = cre[...]
    him_ref[0] = cim[...]


def _ssm_prompt(u, bmat, cmat, tab, d_row, wglu, bglu, ts, lc=512, job=None):
    bsz, seq, _ = u.shape
    out_shape = (
        jax.ShapeDtypeStruct((bsz, seq, D_SSM), BF16),
        jax.ShapeDtypeStruct((bsz, SUBLANES, N_STATE), F32),
        jax.ShapeDtypeStruct((bsz, SUBLANES, N_STATE), F32),
    )
    in_specs = [
        pl.BlockSpec((1, ts, D_SSM), lambda b, l, *_: (b, l, 0)),
        _const_spec(bmat.shape), _const_spec(cmat.shape), _const_spec(tab.shape),
        _const_spec(d_row.shape), _const_spec(wglu.shape), _const_spec(bglu.shape),
    ]
    out_specs = (
        pl.BlockSpec((1, ts, D_SSM), lambda b, l, *_: (b, l, 0)),
        pl.BlockSpec((1, SUBLANES, N_STATE), lambda b, l, *_: (b, 0, 0)),
        pl.BlockSpec((1, SUBLANES, N_STATE), lambda b, l, *_: (b, 0, 0)),
    )
    return _tile_call(
        functools.partial(_ssm_prompt_kernel, ts=ts, lc=lc),
        grid=(bsz, seq // ts), in_specs=in_specs, out_specs=out_specs, out_shape=out_shape,
        scratch_shapes=[pltpu.VMEM((ts, N_STATE), F32), pltpu.VMEM((ts, N_STATE), F32),
                        pltpu.VMEM((SUBLANES, N_STATE), F32), pltpu.VMEM((SUBLANES, N_STATE), F32)],
        args=(u, bmat, cmat, tab, d_row, wglu, bglu), name="ssm_prompt", job=job)


def _ssm_sample_kernel(u_ref, h0r_ref, h0i_ref, ar_ref, ai_ref, bmat_ref, cmat_ref, d_ref, wglu_ref,
                       bglu_ref, y_ref, hr_ref, hi_ref):
    u = u_ref[...]
    ub = u.astype(BF16)
    parts = [_ssm_bu(ub, bmat_ref, j) for j in range(SSM_CHUNKS)]
    ar = ar_ref[...]
    ai = ai_ref[...]
    h0r = h0r_ref[...]
    h0i = h0i_ref[...]
    hr = jnp.concatenate([p[0] for p in parts], axis=1) + (ar * h0r - ai * h0i)
    hi = jnp.concatenate([p[1] for p in parts], axis=1) + (ar * h0i + ai * h0r)
    hr_ref[...] = hr
    hi_ref[...] = hi
    y_ref[...] = _ssm_out(hr, hi, u, cmat_ref, d_ref, wglu_ref, bglu_ref).astype(y_ref.dtype)


def _ssm_sample(u, h0r, h0i, a_re, a_im, bmat, cmat, d_row, wglu, bglu):
    n = u.shape[0]
    args = (u, h0r, h0i, a_re, a_im, bmat, cmat, d_row, wglu, bglu)
    return pl.pallas_call(
        _ssm_sample_kernel,
        grid=(1,), in_specs=[_const_spec(a.shape) for a in args],
        out_specs=(_const_spec((n, D_SSM)), _const_spec((n, N_STATE)), _const_spec((n, N_STATE))),
        out_shape=(jax.ShapeDtypeStruct((n, D_SSM), BF16), jax.ShapeDtypeStruct((n, N_STATE), F32),
                   jax.ShapeDtypeStruct((n, N_STATE), F32)),
        compiler_params=_params(("arbitrary",)),
        name="ssm_sample",
    )(*args)


def _attn_prompt_kernel(qaug_ref, kaug_ref, vt_ref, o_ref, s_ref, *, tq, heads):
    qi = pl.program_id(2)

    def qk(kb, slot, hh):
        k0 = pl.multiple_of(kb * tq, tq)
        s_ref[slot, hh] = _dot(kaug_ref[0, hh, pl.ds(k0, tq), :], qaug_ref[0, hh])

    def softmax_pv(kb, slot, hh, state, masked):
        m, acc = state
        k0 = pl.multiple_of(kb * tq, tq)
        s = s_ref[slot, hh]
        if masked:
            krow = lax.broadcasted_iota(jnp.int32, (tq, tq), 0)
            qcol = lax.broadcasted_iota(jnp.int32, (tq, tq), 1)
            s = jnp.where(krow <= qcol, s, NEG_BIG)
        m_new = jnp.maximum(m, jnp.max(s, axis=0, keepdims=True))
        p = jnp.exp2(s - m_new).astype(BF16)
        pv = _dot(vt_ref[0, hh, :, pl.ds(k0, tq)], p)
        return m_new, jnp.exp2(m - m_new) * acc + pv

    def step(kb, cur, states):
        out = []
        qk(kb + 1, 1 - cur, 0)
        for hh in range(heads):
            if hh + 1 < heads:
                qk(kb + 1, 1 - cur, hh + 1)
            out.append(softmax_pv(kb, cur, hh, states[hh], False))
        return tuple(out)

    def finish(cur, states):
        outs = []
        for hh in range(heads):
            _, acc = softmax_pv(qi, cur, hh, states[hh], True)
            outs.append((acc[0:HEAD_DIM] / acc[HEAD_DIM:HEAD_DIM + 1]).T)
        o_ref[0] = jnp.concatenate(outs, axis=1).astype(o_ref.dtype)

    for hh in range(heads):
        qk(0, 0, hh)
    init = tuple((jnp.full((1, tq), NEG_BIG, F32), jnp.zeros((VAUG, tq), F32)) for _ in range(heads))
    states = lax.fori_loop(0, qi // 2, lambda i, st: step(2 * i + 1, 1, step(2 * i, 0, st)), init)
    odd = lax.rem(qi, 2) == 1

    @pl.when(odd)
    def _():
        finish(1, step(qi - 1, 0, states))

    @pl.when(jnp.logical_not(odd))
    def _():
        finish(0, states)


def _attn_prompt(qaug, kaug, vtb, tq, heads):
    bsz, _, _, seq = qaug.shape
    width = heads * HEAD_DIM
    in_specs = [
        pl.BlockSpec((1, heads, KAUG, tq), lambda b, h, q: (b, h, 0, q)),
        pl.BlockSpec((1, heads, seq, KAUG), lambda b, h, q: (b, h, 0, 0)),
        pl.BlockSpec((1, heads, VAUG, seq), lambda b, h, q: (b, h, 0, 0)),
    ]
    return pl.pallas_call(
        functools.partial(_attn_prompt_kernel, tq=tq, heads=heads),
        grid=(bsz, N_HEADS // heads, seq // tq),
        in_specs=in_specs,
        out_specs=pl.BlockSpec((1, tq, width), lambda b, h, q: (b, q, h)),
        out_shape=jax.ShapeDtypeStruct((bsz, seq, D_ATT), BF16),
        scratch_shapes=[pltpu.VMEM((2, heads, tq, tq), F32)],
        compiler_params=_params(("arbitrary", "arbitrary", "arbitrary")),
        name="attn_prompt",
    )(qaug, kaug, vtb)


def _mix_kernel(x_ref, ys_ref, ya_ref, wgs_ref, wga_ref, wbs_ref, wba_ref, wo_ref, g_ref, b_ref, o_ref, *,
                overlap):
    overlap()
    x = x_ref[0]
    xb = x.astype(BF16)
    gs = jax.nn.sigmoid(_dot(xb, wgs_ref[...]))
    ga = jax.nn.sigmoid(_dot(xb, wga_ref[...]))
    mixed = gs * _dot(ys_ref[0], wbs_ref[...]) + ga * _dot(ya_ref[0], wba_ref[...])
    t = ALPHA * x + _dot(mixed.astype(BF16), wo_ref[...])
    o_ref[0] = _layer_norm(t, g_ref[...], b_ref[...])


def _mix(x, ys, ya, wgs, wga, wbs, wba, wo, g, b, tm, job=None):
    bsz, seq, _ = x.shape
    ws = (wgs, wga, wbs, wba, wo, g, b)
    in_specs = [
        pl.BlockSpec((1, tm, D_MODEL), lambda i, l, *_: (i, l, 0)),
        pl.BlockSpec((1, tm, D_SSM), lambda i, l, *_: (i, l, 0)),
        pl.BlockSpec((1, tm, D_ATT), lambda i, l, *_: (i, l, 0)),
    ] + [_const_spec(w.shape) for w in ws]
    return _tile_call(
        _mix_kernel, grid=(bsz, seq // tm), in_specs=in_specs,
        out_specs=(pl.BlockSpec((1, tm, D_MODEL), lambda i, l, *_: (i, l, 0)),),
        out_shape=(jax.ShapeDtypeStruct((bsz, seq, D_MODEL), F32),),
        scratch_shapes=[], args=(x, ys, ya, *ws), name="mix_ln1", job=job)


def _ffn_tail(x, a, am1, am2, bgate, cw_ref, cb_ref, wdn_ref, g_ref, b_ref, wpg_ref, bpg_ref, wple_ref, p):
    c = cb_ref[...] + cw_ref[0:1, :] * am2 + cw_ref[1:2, :] * am1 + cw_ref[2:3, :] * a
    hcat = (jax.nn.gelu(c) * bgate).astype(BF16)
    x2 = _layer_norm(ALPHA * x + _dot(hcat, wdn_ref[...]), g_ref[...], b_ref[...])
    gate = jax.nn.sigmoid(_dot(x2.astype(BF16), wpg_ref[...]) + bpg_ref[...])
    return x2 + gate * _dot(p.astype(BF16), wple_ref[...])


def _ffn_prompt_kernel(x_ref, p_ref, wup_ref, cw_ref, cb_ref, wdn_ref, g_ref, b_ref, wpg_ref, bpg_ref,
                       wple_ref, o_ref, tail_ref, prev_ref, *, tm, overlap):
    @pl.when(pl.program_id(1) == 0)
    def _():
        prev_ref[...] = jnp.zeros_like(prev_ref)

    overlap()
    x = x_ref[0]
    up = _dot(x.astype(BF16), wup_ref[...])
    a = up[:, :D_FF]
    row = lax.broadcasted_iota(jnp.int32, (tm, D_FF), 0)
    p1 = jnp.broadcast_to(prev_ref[SUBLANES - 1:SUBLANES, :], (tm, D_FF))
    p2 = jnp.broadcast_to(prev_ref[SUBLANES - 2:SUBLANES - 1, :], (tm, D_FF))
    am1 = jnp.where(row == 0, p1, pltpu.roll(a, 1, 0))
    am2 = jnp.where(row == 0, p2, jnp.where(row == 1, p1, pltpu.roll(a, 2, 0)))
    tail = a[tm - SUBLANES:, :]
    prev_ref[...] = tail
    tail_ref[0] = tail
    o_ref[0] = _ffn_tail(x, a, am1, am2, up[:, D_FF:], cw_ref, cb_ref, wdn_ref, g_ref, b_ref,
                         wpg_ref, bpg_ref, wple_ref, p_ref[0])


def _ffn_sample_kernel(x_ref, p_ref, s0_ref, s1_ref, wup_ref, cw_ref, cb_ref, wdn_ref, g_ref, b_ref,
                       wpg_ref, bpg_ref, wple_ref, o_ref, a_ref):
    x = x_ref[...]
    up = _dot(x.astype(BF16), wup_ref[...])
    a = up[:, :D_FF]
    a_ref[...] = a
    o_ref[...] = _ffn_tail(x, a, s1_ref[...], s0_ref[...], up[:, D_FF:], cw_ref, cb_ref, wdn_ref, g_ref,
                           b_ref, wpg_ref, bpg_ref, wple_ref, p_ref[...])


def _ffn_prompt(x, p, ws, tm, job=None):
    bsz, seq, _ = x.shape
    in_specs = [
        pl.BlockSpec((1, tm, D_MODEL), lambda i, l, *_: (i, l, 0)),
        pl.BlockSpec((1, tm, p.shape[-1]), lambda i, l, *_: (i, l, 0)),
    ] + [_resident_spec(w.shape) for w in ws]
    return _tile_call(
        functools.partial(_ffn_prompt_kernel, tm=tm), grid=(bsz, seq // tm), in_specs=in_specs,
        out_specs=(pl.BlockSpec((1, tm, D_MODEL), lambda i, l, *_: (i, l, 0)),
                   pl.BlockSpec((1, SUBLANES, D_FF), lambda i, l, *_: (i, 0, 0))),
        out_shape=(jax.ShapeDtypeStruct((bsz, seq, D_MODEL), F32),
                   jax.ShapeDtypeStruct((bsz, SUBLANES, D_FF), F32)),
        scratch_shapes=[pltpu.VMEM((SUBLANES, D_FF), F32)],
        args=(x, p, *ws), name="ffn_prompt", job=job)


def _ffn_sample(x, p, s0, s1, ws):
    n = x.shape[0]
    args = (x, p, s0, s1) + tuple(ws)
    return pl.pallas_call(
        _ffn_sample_kernel, grid=(1,), in_specs=[_const_spec(a.shape) for a in args],
        out_specs=(_const_spec((n, D_MODEL)), _const_spec((n, D_FF))),
        out_shape=(jax.ShapeDtypeStruct((n, D_MODEL), F32), jax.ShapeDtypeStruct((n, D_FF), F32)),
        compiler_params=_params(("arbitrary",)),
        name="ffn_sample",
    )(*args)


def _inproj_sample_kernel(x_ref, wu_ref, wt_ref, bfc_ref, u_ref, qt_ref, kt_ref, vt_ref, lft_ref):
    xb = x_ref[...].astype(BF16)
    u_ref[...] = _dot(xb, wu_ref[...])
    zt = _dot_nt(wt_ref[...], xb)
    qt_ref[...] = zt[0:D_ATT, :]
    kt_ref[...] = zt[D_ATT:2 * D_ATT, :]
    vt_ref[...] = zt[2 * D_ATT:3 * D_ATT, :]
    lft_ref[...] = _log_sigmoid(zt[3 * D_ATT:3 * D_ATT + N_HEADS, :] + bfc_ref[:, 0:1])


def _inproj_sample(x, wu, wt, bf_col):
    n = x.shape[0]
    args = (x, wu, wt, bf_col)
    shp = lambda r: jax.ShapeDtypeStruct((r, n), F32)
    return pl.pallas_call(
        _inproj_sample_kernel, grid=(1,), in_specs=[_const_spec(a.shape) for a in args],
        out_specs=(_const_spec((n, D_SSM)), _const_spec((D_ATT, n)), _const_spec((D_ATT, n)),
                   _const_spec((D_ATT, n)), _const_spec((N_HEADS, n))),
        out_shape=(jax.ShapeDtypeStruct((n, D_SSM), F32), shp(D_ATT), shp(D_ATT), shp(D_ATT), shp(N_HEADS)),
        compiler_params=_params(("arbitrary",)),
        name="inproj_sample",
    )(*args)


def _decode_only(job):
    def body(*, overlap):
        overlap()

    return _tile_call(body, grid=(job.count, job.steps_per_seq), in_specs=[], out_specs=(), out_shape=(),
                      scratch_shapes=[], args=(), name="decode_attn", job=job)[0]


def _discretize(a_re, a_im, log_dt, b_re, b_im):
    dt = jnp.exp(log_dt)[:, None]
    mag = jnp.exp(dt * a_re)
    ang = dt * a_im
    e_re = mag * jnp.cos(ang)
    e_im = mag * jnp.sin(ang)
    nr = e_re - 1.0
    ni = e_im
    den = a_re * a_re + a_im * a_im
    z_re = ((nr * a_re + ni * a_im) / den)[..., None]
    z_im = ((ni * a_re - nr * a_im) / den)[..., None]
    return e_re, e_im, z_re * b_re - z_im * b_im, z_re * b_im + z_im * b_re


def _block_diag(blocks):
    g, r, c = blocks.shape
    eye = jnp.eye(g, dtype=blocks.dtype)
    return (eye[:, None, :, None] * blocks[:, :, None, :]).reshape(g * r, g * c)


def _cmul(ar, ai, br, bi):
    return ar * br - ai * bi, ar * bi + ai * br


def _scan_tables(e_re, e_im):
    a1r, a1i = e_re.reshape(1, N_STATE), e_im.reshape(1, N_STATE)
    a2r, a2i = _cmul(a1r, a1i, a1r, a1i)
    a4r, a4i = _cmul(a2r, a2i, a2r, a2i)
    rows = jnp.arange(SUBLANES)[:, None]
    tabs_r, tabs_i = [], []
    for k, (pr, pi) in zip((1, 2, 4), ((a1r, a1i), (a2r, a2i), (a4r, a4i))):
        tabs_r.append(jnp.where(rows >= k, pr, 0.0))
        tabs_i.append(jnp.where(rows >= k, pi, 0.0))
    pw_r, pw_i = [a1r], [a1i]
    for _ in range(SUBLANES - 1):
        nr, ni = _cmul(pw_r[-1], pw_i[-1], a1r, a1i)
        pw_r.append(nr)
        pw_i.append(ni)
    return jnp.stack(tabs_r + tabs_i + [jnp.concatenate(pw_r, 0), jnp.concatenate(pw_i, 0)])


def kernel(x_prompt, x_sample, p_prompt, p_sample, cache_k, cache_v, cache_logf, page_table, state_ssm_re, state_ssm_im, state_conv, w_in, b_f, ssm_a_re, ssm_a_im, ssm_log_dt, ssm_b_re, ssm_b_im, ssm_c_re, ssm_c_im, ssm_d, w_glu, b_glu, w_br_ssm, w_br_att, w_o, ln1_g, ln1_b, w_up, conv_w, conv_b, w_down, ln2_g, ln2_b, w_ple_gate, b_ple_gate, w_ple):
    assert w_in.shape[0] == DEPTH
    bsz, seq, _ = x_prompt.shape
    nb = x_sample.shape[0]
    tm = 256

    win = w_in[0]
    o = [0, D_SSM, D_SSM + D_ATT, D_SSM + 2 * D_ATT, D_SSM + 3 * D_ATT, D_SSM + 3 * D_ATT + N_HEADS]
    wu, wq, wk, wv, wf = (win[:, o[i]:o[i + 1]] for i in range(5))
    wgs = win[:, o[5]:o[5] + D_MODEL].astype(BF16)
    wga = win[:, o[5] + D_MODEL:].astype(BF16)
    scale = HEAD_DIM ** -0.5
    wk_pad = jnp.pad(wk.reshape(D_MODEL, N_HEADS, HEAD_DIM), ((0, 0), (0, 0), (0, KAUG - HEAD_DIM)))
    wf_pad = jnp.pad(wf, ((0, 0), (0, LANES - N_HEADS)))
    w1 = jnp.concatenate([wu, wk_pad.reshape(D_MODEL, N_HEADS * KAUG), wf_pad], axis=1).astype(BF16)
    wt = jnp.concatenate([wq * scale, wk, wv, wf], axis=1).T.astype(BF16)
    wu_b = wu.astype(BF16)
    bf_row = jnp.pad(b_f, ((0, 0), (0, LANES - N_HEADS)))
    bf_col = jnp.broadcast_to(b_f[0][:, None], (N_HEADS, LANES))
    tri = jnp.tri(tm, dtype=BF16)
    prow = jnp.arange(3 * LANES)[:, None]
    pcol = jnp.arange(N_HEADS * KAUG)[None, :]
    place = jnp.where((prow % LANES < N_HEADS) & (pcol == (prow % LANES) * KAUG + HEAD_DIM + prow // LANES),
                      1.0, 0.0).astype(BF16)

    e_re, e_im, bb_re, bb_im = _discretize(ssm_a_re[0], ssm_a_im[0], ssm_log_dt[0], ssm_b_re[0], ssm_b_im[0])
    gpc = N_GROUPS // SSM_CHUNKS
    chunks = [slice(j * gpc, (j + 1) * gpc) for j in range(SSM_CHUNKS)]
    bt_re, bt_im = jnp.swapaxes(bb_re, 1, 2), jnp.swapaxes(bb_im, 1, 2)
    ct_re, ct_im = jnp.swapaxes(ssm_c_re[0], 1, 2), jnp.swapaxes(-ssm_c_im[0], 1, 2)
    bmat = jnp.stack([jnp.concatenate([_block_diag(bt_re[c]), _block_diag(bt_im[c])], axis=1)
                      for c in chunks]).astype(BF16)
    cmat = jnp.stack([jnp.concatenate([_block_diag(ct_re[c]), _block_diag(ct_im[c])], axis=0)
                      for c in chunks]).astype(BF16)
    tab = _scan_tables(e_re, e_im)
    a_re_row, a_im_row = e_re.reshape(1, N_STATE), e_im.reshape(1, N_STATE)
    wglu = w_glu[0].astype(BF16)
    mix_ws = (wgs, wga, w_br_ssm[0].astype(BF16), w_br_att[0].astype(BF16), w_o[0].astype(BF16), ln1_g, ln1_b)
    ffn_ws = (w_up[0].astype(BF16), conv_w[0], conv_b, w_down[0].astype(BF16), ln2_g, ln2_b,
              w_ple_gate[0].astype(BF16), b_ple_gate, w_ple[0].astype(BF16))

    xs = x_sample[:, 0]
    us, qts, kts, vts, lfts = _inproj_sample(xs, wu_b, wt, bf_col)
    kc = jnp.transpose(cache_k[0], (0, 2, 3, 1))
    vc = jnp.transpose(cache_v[0], (0, 2, 3, 1))
    lfc = jnp.transpose(cache_logf[0], (0, 2, 1))
    shp3 = (N_HEADS, HEAD_DIM, nb)
    todo = _DecodeJob(page_table.reshape(-1), kc, vc, lfc, qts.reshape(shp3), kts.reshape(shp3),
                      vts.reshape(shp3), lfts, 0, nb)
    host_steps = bsz * (seq // tm)
    decoded = []

    def hosted(call, *args, **kwargs):
        nonlocal todo
        count = host_steps // todo.steps_per_seq
        if count * todo.steps_per_seq != host_steps or count > todo.count:
            return call(*args, **kwargs)
        *outs, dec = call(*args, **kwargs, job=todo._replace(count=count))
        decoded.append(dec)
        todo = todo._replace(first=todo.first + count, count=todo.count - count)
        return outs

    u, kaug, qaug, kt, vt, vtb, lft = hosted(_inproj_prompt, x_prompt, w1, wt, bf_row, bf_col, tri, place, tm)
    ys, hre, him = hosted(_ssm_prompt, u, bmat, cmat, tab, ssm_d, wglu, b_glu, ts=tm)
    ya = _attn_prompt(qaug, kaug, vtb, tq=512, heads=4)
    x1, = hosted(_mix, x_prompt, ys, ya, *mix_ws, tm=tm)
    y_prompt, tail = hosted(_ffn_prompt, x1, p_prompt[0], ffn_ws, tm=tm)
    if todo.count:
        decoded.append(_decode_only(todo))

    k_prompt = jnp.transpose(kt.reshape(bsz, N_HEADS, HEAD_DIM, seq), (0, 3, 1, 2))[None]
    v_prompt = jnp.transpose(vt.reshape(bsz, N_HEADS, HEAD_DIM, seq), (0, 3, 1, 2))[None]
    logf_prompt = jnp.transpose(lft, (0, 2, 1))[None]
    ssm_re_prompt = hre[:, 0].reshape(1, bsz, N_GROUPS, STATE_DIM)
    ssm_im_prompt = him[:, 0].reshape(1, bsz, N_GROUPS, STATE_DIM)
    conv_prompt = tail[:, SUBLANES - (CONV_W - 1):][None]

    ys_s, hr_s, hi_s = _ssm_sample(us, state_ssm_re[0].reshape(nb, N_STATE), state_ssm_im[0].reshape(nb, N_STATE),
                                   a_re_row, a_im_row, bmat, cmat, ssm_d, wglu, b_glu)
    yat = functools.reduce(lambda x, y: x + y, decoded)
    ya_s = yat.reshape(D_ATT, nb).T.astype(BF16)
    x1_s = _mix(xs[None], ys_s[None], ya_s[None], *mix_ws, tm=nb)[0][0]
    y_s, a_s = _ffn_sample(x1_s, p_sample[0, :, 0], state_conv[0, :, 0], state_conv[0, :, 1], ffn_ws)

    y_sample = y_s[:, None, :]
    k_sample = kts.T.reshape(1, nb, 1, N_HEADS, HEAD_DIM)
    v_sample = vts.T.reshape(1, nb, 1, N_HEADS, HEAD_DIM)
    logf_sample = lfts.T.reshape(1, nb, 1, N_HEADS)
    ssm_re_sample = hr_s.reshape(1, nb, N_GROUPS, STATE_DIM)
    ssm_im_sample = hi_s.reshape(1, nb, N_GROUPS, STATE_DIM)
    conv_sample = jnp.stack([state_conv[0, :, 1], a_s], axis=1)[None]

    return (y_prompt, y_sample, k_prompt, v_prompt, logf_prompt, ssm_re_prompt, ssm_im_prompt, conv_prompt,
            k_sample, v_sample, logf_sample, ssm_re_sample, ssm_im_sample, conv_sample)
```

```python
import functools
from typing import NamedTuple

import jax
import jax.numpy as jnp
from jax import lax
from jax.experimental import pallas as pl
from jax.experimental.pallas import tpu as pltpu

F32 = jnp.float32
BF16 = jnp.bfloat16

D_MODEL = 1024
D_SSM = 512
SSM_GROUP = 16
N_GROUPS = 32
STATE_DIM = 64
N_STATE = N_GROUPS * STATE_DIM
N_HEADS = 8
HEAD_DIM = 64
D_ATT = 512
D_FF = 2816
CONV_W = 3
DEPTH = 1
ALPHA = (2 * DEPTH) ** 0.25
LN_EPS = 1e-5
NEG_BIG = -1e30
LANES = 128
SUBLANES = 8
KAUG = 128
VAUG = 80
LOG2E = 1.4426950408889634
VMEM_LIMIT = 56 * 1024 * 1024


def _dot(a, b):
    return jnp.dot(a, b, preferred_element_type=F32)


def _dot_nt(a, b):
    return lax.dot_general(a, b, (((1,), (1,)), ((), ())), preferred_element_type=F32)


def _log_sigmoid(x):
    return jnp.minimum(x, 0.0) - jnp.log1p(jnp.exp(-jnp.abs(x)))


def _split3(x):
    hi = x.astype(BF16)
    r1 = x - hi.astype(F32)
    mid = r1.astype(BF16)
    lo = (r1 - mid.astype(F32)).astype(BF16)
    return hi, mid, lo


def _layer_norm(t, g, b):
    mu = jnp.mean(t, axis=-1, keepdims=True)
    c = t - mu
    var = jnp.mean(c * c, axis=-1, keepdims=True)
    return c * lax.rsqrt(var + LN_EPS) * g + b


def _params(sem):
    return pltpu.CompilerParams(dimension_semantics=sem, vmem_limit_bytes=VMEM_LIMIT)


def _const_spec(shape):
    nd = len(shape)
    return pl.BlockSpec(shape, lambda *_: (0,) * nd)


def _resident_spec(shape):
    nd = len(shape)
    return pl.BlockSpec(shape, lambda *_: (0,) * nd, pipeline_mode=pl.Buffered(1))


DEC_PAGES = 16


class _DecodeJob(NamedTuple):
    page_flat: jax.Array
    kc: jax.Array
    vc: jax.Array
    lfc: jax.Array
    qt: jax.Array
    ktn: jax.Array
    vtn: jax.Array
    lfn: jax.Array
    first: int
    count: int

    @property
    def pages_per_seq(self):
        return self.page_flat.shape[0] // self.qt.shape[-1]

    @property
    def steps_per_seq(self):
        return self.pages_per_seq // DEC_PAGES


class _DecodeRefs(NamedTuple):
    k: tuple
    v: tuple
    lf: tuple
    qt: object
    ktn: object
    vtn: object
    lfn: object
    out: object
    qb: object
    m: object
    l: object
    acc: object
    carry: object


def _dec_column(x, e):
    lane = lax.broadcasted_iota(jnp.int32, x.shape, 1)
    col = jnp.sum(jnp.where(lane == e, x, 0.0), axis=1, keepdims=True)
    return jnp.broadcast_to(col, (x.shape[0], LANES))


def _dec_begin(d, e, jj, first_step):
    @pl.when(first_step)
    def _():
        d.out[...] = jnp.zeros_like(d.out)

    @pl.when(jj == 0)
    def _():
        for h in range(N_HEADS):
            d.qb[h] = _dec_column(d.qt[h], e)
        d.m[...] = jnp.full_like(d.m, NEG_BIG)
        d.l[...] = jnp.zeros_like(d.l)
        d.acc[...] = jnp.zeros_like(d.acc)
        d.carry[...] = _dec_column(d.lfn[...], e)


def _dec_main(d):
    pg = len(d.k)
    sub = lax.broadcasted_iota(jnp.int32, (N_HEADS, LANES), 0)
    lf_all = jnp.concatenate([d.lf[i][0] for i in range(pg)], axis=0)
    row = lax.broadcasted_iota(jnp.int32, (LANES, 2 * LANES), 0)
    col = lax.broadcasted_iota(jnp.int32, (LANES, 2 * LANES), 1)
    later = jnp.where(jnp.logical_or(row > col, col >= LANES), 1.0, 0.0).astype(BF16)
    sums = _dot(jnp.concatenate(_split3(lf_all), axis=0), later)
    n = pg * N_HEADS
    sums = (sums[0:n] + sums[n:2 * n]) + sums[2 * n:3 * n]
    carry = d.carry[...]
    logits = [None] * pg
    for i in reversed(range(pg)):
        logits[i] = carry + sums[i * N_HEADS:(i + 1) * N_HEADS, 0:LANES]
        carry = carry + sums[i * N_HEADS:(i + 1) * N_HEADS, LANES:2 * LANES]
    d.carry[...] = carry

    for h in range(N_HEADS):
        qh = d.qb[h]
        for i in range(pg):
            r = jnp.sum(d.k[i][0, h] * qh, axis=0, keepdims=True)
            logits[i] = logits[i] + jnp.where(sub == h, jnp.broadcast_to(r, logits[i].shape), 0.0)
    m_old = d.m[...]
    m_new = m_old
    for s in logits:
        m_new = jnp.maximum(m_new, s)
    alpha = jnp.exp(m_old - m_new)
    probs = [jnp.exp(s - m_new) for s in logits]
    d.l[...] = alpha * d.l[...] + functools.reduce(lambda x, y: x + y, probs)
    d.m[...] = m_new
    for h in range(N_HEADS):
        acc = d.acc[h] * jnp.broadcast_to(alpha[h:h + 1, :], (HEAD_DIM, LANES))
        for i in range(pg):
            acc = acc + d.v[i][0, h] * jnp.broadcast_to(probs[i][h:h + 1, :], (HEAD_DIM, LANES))
        d.acc[h] = acc


def _dec_end(d, e, jj, steps_per_seq):
    @pl.when(jj == steps_per_seq - 1)
    def _():
        sub = lax.broadcasted_iota(jnp.int32, (N_HEADS, LANES), 0)
        m = d.m[...]
        l = d.l[...]
        s_new = jnp.zeros((N_HEADS, LANES), F32)
        for h in range(N_HEADS):
            r = jnp.sum(_dec_column(d.ktn[h], e) * d.qb[h], axis=0, keepdims=True)
            s_new = s_new + jnp.where(sub == h, jnp.broadcast_to(r, s_new.shape), 0.0)
        m_tot = jnp.maximum(jnp.broadcast_to(jnp.max(m, axis=1, keepdims=True), m.shape), s_new)
        w = jnp.exp(m - m_tot)
        p_new = jnp.exp(s_new - m_tot)
        l_tot = jnp.broadcast_to(jnp.sum(l * w, axis=1, keepdims=True), l.shape) + p_new
        inv = 1.0 / l_tot
        olane = lax.broadcasted_iota(jnp.int32, (HEAD_DIM, d.out.shape[-1]), 1)
        for h in range(N_HEADS):
            wh = jnp.broadcast_to(w[h:h + 1, :], (HEAD_DIM, LANES))
            num = jnp.sum(d.acc[h] * wh, axis=1, keepdims=True)
            num = jnp.broadcast_to(num, (HEAD_DIM, LANES)) + _dec_column(d.vtn[h], e) * jnp.broadcast_to(
                p_new[h:h + 1, :], (HEAD_DIM, LANES))
            val = num * jnp.broadcast_to(inv[h:h + 1, :], (HEAD_DIM, LANES))
            d.out[h] = jnp.where(olane == e, val, d.out[h])


def _tile_call(body, *, grid, in_specs, out_specs, out_shape, scratch_shapes, args, name, job=None):
    sem = ("arbitrary",) * len(grid)
    if job is None:
        return pl.pallas_call(
            functools.partial(body, overlap=lambda: None), grid=grid, in_specs=in_specs, out_specs=out_specs,
            out_shape=out_shape, scratch_shapes=scratch_shapes, compiler_params=_params(sem), name=name)(*args)

    nb = job.qt.shape[-1]
    page = job.kc.shape[-1]
    assert page == LANES and nb == LANES
    pps, sps = job.pages_per_seq, job.steps_per_seq
    n_steps = 1
    for g in grid:
        n_steps *= g
    assert n_steps == job.count * sps, (n_steps, job.count, sps)

    def locate(ids):
        step = ids[0]
        for g, i in zip(grid[1:], ids[1:]):
            step = step * g + i
        return job.first + lax.div(step, jnp.int32(sps)), lax.rem(step, jnp.int32(sps)), step == 0

    def page_index(i, trailing):
        def index_map(*a):
            e, jj, _ = locate(a[:-1])
            return (a[-1][e * pps + (sps - 1 - jj) * DEC_PAGES + i],) + (0,) * trailing
        return index_map

    dec_in_specs = ([pl.BlockSpec((1, N_HEADS, HEAD_DIM, page), page_index(i, 3)) for i in range(DEC_PAGES)] * 2
                    + [pl.BlockSpec((1, N_HEADS, page), page_index(i, 2)) for i in range(DEC_PAGES)]
                    + [_const_spec((N_HEADS, HEAD_DIM, nb))] * 3 + [_const_spec((N_HEADS, nb))])
    dec_args = ([job.kc] * DEC_PAGES + [job.vc] * DEC_PAGES + [job.lfc] * DEC_PAGES
                + [job.qt, job.ktn, job.vtn, job.lfn])
    dec_scratch = [pltpu.VMEM((N_HEADS, HEAD_DIM, LANES), F32), pltpu.VMEM((N_HEADS, LANES), F32),
                   pltpu.VMEM((N_HEADS, LANES), F32), pltpu.VMEM((N_HEADS, HEAD_DIM, LANES), F32),
                   pltpu.VMEM((N_HEADS, LANES), F32)]
    n_in, n_out, n_scr = len(in_specs), len(out_specs), len(scratch_shapes)

    def fused(pt_ref, *refs):
        del pt_ref
        host_in, dec_in = refs[:n_in], refs[n_in:n_in + len(dec_in_specs)]
        rest = refs[n_in + len(dec_in_specs):]
        host_out, dec_out, scr = rest[:n_out], rest[n_out], rest[n_out + 1:]
        d = _DecodeRefs(dec_in[:DEC_PAGES], dec_in[DEC_PAGES:2 * DEC_PAGES], dec_in[2 * DEC_PAGES:3 * DEC_PAGES],
                        *dec_in[3 * DEC_PAGES:], dec_out, *scr[n_scr:])
        e, jj, first_step = locate([pl.program_id(a) for a in range(len(grid))])

        def overlap():
            _dec_begin(d, e, jj, first_step)
            _dec_main(d)

        body(*host_in, *host_out, *scr[:n_scr], overlap=overlap)
        _dec_end(d, e, jj, sps)

    grid_spec = pltpu.PrefetchScalarGridSpec(
        num_scalar_prefetch=1, grid=grid, in_specs=list(in_specs) + dec_in_specs,
        out_specs=tuple(out_specs) + (_const_spec((N_HEADS, HEAD_DIM, nb)),),
        scratch_shapes=list(scratch_shapes) + dec_scratch)
    return pl.pallas_call(
        fused, grid_spec=grid_spec,
        out_shape=tuple(out_shape) + (jax.ShapeDtypeStruct((N_HEADS, HEAD_DIM, nb), F32),),
        compiler_params=_params(sem), name=name)(job.page_flat, *args, *dec_args)


def _inproj_prompt_kernel(x_ref, w1_ref, wt_ref, bfr_ref, bfc_ref, tri_ref, place_ref,
                          u_ref, kaug_ref, qaug_ref, kt_ref, vt_ref, vtb_ref, lft_ref, carry_ref, *, tm, overlap):
    @pl.when(pl.program_id(1) == 0)
    def _():
        carry_ref[...] = jnp.zeros_like(carry_ref)

    overlap()
    xb = x_ref[0].astype(BF16)
    z1 = _dot(xb, w1_ref[...])
    u_ref[0] = z1[:, :D_SSM]
    kpad = z1[:, D_SSM:D_SSM + N_HEADS * KAUG]
    f_logit = z1[:, D_SSM + N_HEADS * KAUG:]

    lane = lax.broadcasted_iota(jnp.int32, (tm, LANES), 1)
    lf = jnp.where(lane < N_HEADS, _log_sigmoid(f_logit + bfr_ref[...]), 0.0)
    hi, mid, lo = _split3(lf)
    cum = _dot(tri_ref[...], jnp.concatenate([hi, mid, lo], axis=1))
    fcum = (cum[:, :LANES] + cum[:, LANES:2 * LANES]) + cum[:, 2 * LANES:] + carry_ref[0:1, :]
    carry_ref[...] = jnp.broadcast_to(fcum[tm - 1:tm, :], carry_ref.shape)
    nhi, nmid, nlo = _split3(fcum * (-LOG2E))
    kaug = kpad + _dot(jnp.concatenate([nhi, nmid, nlo], axis=1), place_ref[...])
    kaug = kaug.astype(BF16)
    for h in range(N_HEADS):
        kaug_ref[0, h] = kaug[:, h * KAUG:(h + 1) * KAUG]

    zt = _dot_nt(wt_ref[...], xb)
    row = lax.broadcasted_iota(jnp.int32, (KAUG - HEAD_DIM, tm), 0)
    ones_rows = jnp.where(row < 3, 1.0, 0.0).astype(BF16)
    for h in range(N_HEADS):
        qaug_ref[0, h, 0:HEAD_DIM, :] = (zt[h * HEAD_DIM:(h + 1) * HEAD_DIM, :] * LOG2E).astype(BF16)
        qaug_ref[0, h, HEAD_DIM:KAUG, :] = ones_rows
    kt_ref[0] = zt[D_ATT:2 * D_ATT, :]
    vt = zt[2 * D_ATT:3 * D_ATT, :]
    vt_ref[0] = vt
    vrow = lax.broadcasted_iota(jnp.int32, (VAUG - HEAD_DIM, tm), 0)
    one_row = jnp.where(vrow == 0, 1.0, 0.0).astype(BF16)
    for h in range(N_HEADS):
        vtb_ref[0, h, 0:HEAD_DIM, :] = vt[h * HEAD_DIM:(h + 1) * HEAD_DIM, :].astype(BF16)
        vtb_ref[0, h, HEAD_DIM:VAUG, :] = one_row
    lft_ref[0] = _log_sigmoid(zt[3 * D_ATT:3 * D_ATT + N_HEADS, :] + bfc_ref[:, 0:1])


def _inproj_prompt(x, w1, wt, bf_row, bf_col, tri, place, tm, job=None):
    bsz, seq, _ = x.shape
    out_shape = (
        jax.ShapeDtypeStruct((bsz, seq, D_SSM), F32),
        jax.ShapeDtypeStruct((bsz, N_HEADS, seq, KAUG), BF16),
        jax.ShapeDtypeStruct((bsz, N_HEADS, KAUG, seq), BF16),
        jax.ShapeDtypeStruct((bsz, D_ATT, seq), F32),
        jax.ShapeDtypeStruct((bsz, D_ATT, seq), F32),
        jax.ShapeDtypeStruct((bsz, N_HEADS, VAUG, seq), BF16),
        jax.ShapeDtypeStruct((bsz, N_HEADS, seq), F32),
    )
    in_specs = [
        pl.BlockSpec((1, tm, D_MODEL), lambda b, l, *_: (b, l, 0)),
        _const_spec(w1.shape), _const_spec(wt.shape), _const_spec(bf_row.shape),
        _const_spec(bf_col.shape), _const_spec(tri.shape), _const_spec(place.shape),
    ]
    out_specs = (
        pl.BlockSpec((1, tm, D_SSM), lambda b, l, *_: (b, l, 0)),
        pl.BlockSpec((1, N_HEADS, tm, KAUG), lambda b, l, *_: (b, 0, l, 0)),
        pl.BlockSpec((1, N_HEADS, KAUG, tm), lambda b, l, *_: (b, 0, 0, l)),
        pl.BlockSpec((1, D_ATT, tm), lambda b, l, *_: (b, 0, l)),
        pl.BlockSpec((1, D_ATT, tm), lambda b, l, *_: (b, 0, l)),
        pl.BlockSpec((1, N_HEADS, VAUG, tm), lambda b, l, *_: (b, 0, 0, l)),
        pl.BlockSpec((1, N_HEADS, tm), lambda b, l, *_: (b, 0, l)),
    )
    return _tile_call(
        functools.partial(_inproj_prompt_kernel, tm=tm),
        grid=(bsz, seq // tm), in_specs=in_specs, out_specs=out_specs, out_shape=out_shape,
        scratch_shapes=[pltpu.VMEM((SUBLANES, LANES), F32)],
        args=(x, w1, wt, bf_row, bf_col, tri, place), name="inproj_prompt", job=job)


SSM_CHUNKS = 4
CH_IN = D_SSM // SSM_CHUNKS
CH_ST = N_STATE // SSM_CHUNKS


def _ssm_bu(ub, bmat_ref, j):
    part = _dot(ub[:, j * CH_IN:(j + 1) * CH_IN], bmat_ref[j])
    return part[:, :CH_ST], part[:, CH_ST:]


def _ssm_out(h_re, h_im, u, cmat_ref, d_ref, wglu_ref, bglu_ref):
    ys = []
    for j in range(SSM_CHUNKS):
        st = slice(j * CH_ST, (j + 1) * CH_ST)
        hb = jnp.concatenate([h_re[:, st].astype(BF16), h_im[:, st].astype(BF16)], axis=1)
        ys.append(_dot(hb, cmat_ref[j]))
    y = jnp.concatenate(ys, axis=1) + d_ref[...] * u
    y = jax.nn.gelu(y)
    return y * jax.nn.sigmoid(_dot(y.astype(BF16), wglu_ref[...]) + bglu_ref[...])


def _ssm_prompt_kernel(u_ref, bmat_ref, cmat_ref, tab_ref, d_ref, wglu_ref, bglu_ref,
                       y_ref, hre_ref, him_ref, bre, bim, cre, cim, *, ts, lc, overlap):
    @pl.when(pl.program_id(1) == 0)
    def _():
        cre[...] = jnp.zeros_like(cre)
        cim[...] = jnp.zeros_like(cim)

    overlap()
    u = u_ref[0]
    ub = u.astype(BF16)
    n_cols = N_STATE // lc
    cols = [slice(c * lc, (c + 1) * lc) for c in range(n_cols)]

    def slab(r0, carry, c):
        pr, pi = carry
        xr = bre[pl.ds(r0, SUBLANES), cols[c]]
        xi = bim[pl.ds(r0, SUBLANES), cols[c]]
        for s, k in enumerate((1, 2, 4)):
            mr = tab_ref[s, :, cols[c]]
            mi = tab_ref[3 + s, :, cols[c]]
            sr = pltpu.roll(xr, k, 0)
            si = pltpu.roll(xi, k, 0)
            xr, xi = xr + (mr * sr - mi * si), xi + (mr * si + mi * sr)
        qr = tab_ref[6, :, cols[c]]
        qi = tab_ref[7, :, cols[c]]
        xr, xi = xr + (qr * pr - qi * pi), xi + (qr * pi + qi * pr)
        bre[pl.ds(r0, SUBLANES), cols[c]] = xr
        bim[pl.ds(r0, SUBLANES), cols[c]] = xi
        return (jnp.broadcast_to(xr[SUBLANES - 1:SUBLANES, :], xr.shape),
                jnp.broadcast_to(xi[SUBLANES - 1:SUBLANES, :], xi.shape))

    for j in range(SSM_CHUNKS):
        st = slice(j * CH_ST, (j + 1) * CH_ST)
        bre[:, st], bim[:, st] = _ssm_bu(ub, bmat_ref, j)
    for c in range(n_cols):
        carry = (cre[:, cols[c]], cim[:, cols[c]])
        for i in range(ts // SUBLANES):
            carry = slab(i * SUBLANES, carry, c)
        cre[:, cols[c]], cim[:, cols[c]] = carry
    y_ref[0] = _ssm_out(bre[...], bim[...], u, cmat_ref, d_ref, wglu_ref, bglu_ref).astype(y_ref.dtype)
    hre_ref[0] = cre[...]
    him_ref[0] = cim[...]


def _ssm_prompt(u, bmat, cmat, tab, d_row, wglu, bglu, ts, lc=512, job=None):
    bsz, seq, _ = u.shape
    out_shape = (
        jax.ShapeDtypeStruct((bsz, seq, D_SSM), BF16),
        jax.ShapeDtypeStruct((bsz, SUBLANES, N_STATE), F32),
        jax.ShapeDtypeStruct((bsz, SUBLANES, N_STATE), F32),
    )
    in_specs = [
        pl.BlockSpec((1, ts, D_SSM), lambda b, l, *_: (b, l, 0)),
        _const_spec(bmat.shape), _const_spec(cmat.shape), _const_spec(tab.shape),
        _const_spec(d_row.shape), _const_spec(wglu.shape), _const_spec(bglu.shape),
    ]
    out_specs = (
        pl.BlockSpec((1, ts, D_SSM), lambda b, l, *_: (b, l, 0)),
        pl.BlockSpec((1, SUBLANES, N_STATE), lambda b, l, *_: (b, 0, 0)),
        pl.BlockSpec((1, SUBLANES, N_STATE), lambda b, l, *_: (b, 0, 0)),
    )
    return _tile_call(
        functools.partial(_ssm_prompt_kernel, ts=ts, lc=lc),
        grid=(bsz, seq // ts), in_specs=in_specs, out_specs=out_specs, out_shape=out_shape,
        scratch_shapes=[pltpu.VMEM((ts, N_STATE), F32), pltpu.VMEM((ts, N_STATE), F32),
                        pltpu.VMEM((SUBLANES, N_STATE), F32), pltpu.VMEM((SUBLANES, N_STATE), F32)],
        args=(u, bmat, cmat, tab, d_row, wglu, bglu), name="ssm_prompt", job=job)


def _ssm_sample_kernel(u_ref, h0r_ref, h0i_ref, ar_ref, ai_ref, bmat_ref, cmat_ref, d_ref, wglu_ref,
                       bglu_ref, y_ref, hr_ref, hi_ref):
    u = u_ref[...]
    ub = u.astype(BF16)
    parts = [_ssm_bu(ub, bmat_ref, j) for j in range(SSM_CHUNKS)]
    ar = ar_ref[...]
    ai = ai_ref[...]
    h0r = h0r_ref[...]
    h0i = h0i_ref[...]
    hr = jnp.concatenate([p[0] for p in parts], axis=1) + (ar * h0r - ai * h0i)
    hi = jnp.concatenate([p[1] for p in parts], axis=1) + (ar * h0i + ai * h0r)
    hr_ref[...] = hr
    hi_ref[...] = hi
    y_ref[...] = _ssm_out(hr, hi, u, cmat_ref, d_ref, wglu_ref, bglu_ref).astype(y_ref.dtype)


def _ssm_sample(u, h0r, h0i, a_re, a_im, bmat, cmat, d_row, wglu, bglu):
    n = u.shape[0]
    args = (u, h0r, h0i, a_re, a_im, bmat, cmat, d_row, wglu, bglu)
    return pl.pallas_call(
        _ssm_sample_kernel,
        grid=(1,), in_specs=[_const_spec(a.shape) for a in args],
        out_specs=(_const_spec((n, D_SSM)), _const_spec((n, N_STATE)), _const_spec((n, N_STATE))),
        out_shape=(jax.ShapeDtypeStruct((n, D_SSM), BF16), jax.ShapeDtypeStruct((n, N_STATE), F32),
                   jax.ShapeDtypeStruct((n, N_STATE), F32)),
        compiler_params=_params(("arbitrary",)),
        name="ssm_sample",
    )(*args)


def _attn_prompt_kernel(qaug_ref, kaug_ref, vt_ref, o_ref, s_ref, *, tq, heads, overlap):
    qi = pl.program_id(2)

    def qk(kb, slot, hh):
        k0 = pl.multiple_of(kb * tq, tq)
        s_ref[slot, hh] = _dot(kaug_ref[0, hh, pl.ds(k0, tq), :], qaug_ref[0, hh])

    def softmax_pv(kb, slot, hh, state, masked):
        m, acc = state
        k0 = pl.multiple_of(kb * tq, tq)
        s = s_ref[slot, hh]
        if masked:
            krow = lax.broadcasted_iota(jnp.int32, (tq, tq), 0)
            qcol = lax.broadcasted_iota(jnp.int32, (tq, tq), 1)
            s = jnp.where(krow <= qcol, s, NEG_BIG)
        m_new = jnp.maximum(m, jnp.max(s, axis=0, keepdims=True))
        p = jnp.exp2(s - m_new).astype(BF16)
        pv = _dot(vt_ref[0, hh, :, pl.ds(k0, tq)], p)
        return m_new, jnp.exp2(m - m_new) * acc + pv

    def step(kb, cur, states):
        out = []
        qk(kb + 1, 1 - cur, 0)
        for hh in range(heads):
            if hh + 1 < heads:
                qk(kb + 1, 1 - cur, hh + 1)
            out.append(softmax_pv(kb, cur, hh, states[hh], False))
        return tuple(out)

    def finish(cur, states):
        outs = []
        for hh in range(heads):
            _, acc = softmax_pv(qi, cur, hh, states[hh], True)
            outs.append((acc[0:HEAD_DIM] / acc[HEAD_DIM:HEAD_DIM + 1]).T)
        o_ref[0] = jnp.concatenate(outs, axis=1).astype(o_ref.dtype)

    overlap()
    for hh in range(heads):
        qk(0, 0, hh)
    init = tuple((jnp.full((1, tq), NEG_BIG, F32), jnp.zeros((VAUG, tq), F32)) for _ in range(heads))
    states = lax.fori_loop(0, qi // 2, lambda i, st: step(2 * i + 1, 1, step(2 * i, 0, st)), init)
    odd = lax.rem(qi, 2) == 1

    @pl.when(odd)
    def _():
        finish(1, step(qi - 1, 0, states))

    @pl.when(jnp.logical_not(odd))
    def _():
        finish(0, states)


def _attn_prompt(qaug, kaug, vtb, tq, heads, job=None):
    bsz, _, _, seq = qaug.shape
    width = heads * HEAD_DIM
    in_specs = [
        pl.BlockSpec((1, heads, KAUG, tq), lambda b, h, q, *_: (b, h, 0, q)),
        pl.BlockSpec((1, heads, seq, KAUG), lambda b, h, q, *_: (b, h, 0, 0), pipeline_mode=pl.Buffered(1)),
        pl.BlockSpec((1, heads, VAUG, seq), lambda b, h, q, *_: (b, h, 0, 0), pipeline_mode=pl.Buffered(1)),
    ]
    return _tile_call(
        functools.partial(_attn_prompt_kernel, tq=tq, heads=heads),
        grid=(bsz, N_HEADS // heads, seq // tq), in_specs=in_specs,
        out_specs=(pl.BlockSpec((1, tq, width), lambda b, h, q, *_: (b, q, h)),),
        out_shape=(jax.ShapeDtypeStruct((bsz, seq, D_ATT), BF16),),
        scratch_shapes=[pltpu.VMEM((2, heads, tq, tq), F32)],
        args=(qaug, kaug, vtb), name="attn_prompt", job=job)


def _mix_kernel(x_ref, ys_ref, ya_ref, wgs_ref, wga_ref, wbs_ref, wba_ref, wo_ref, g_ref, b_ref, o_ref, *,
                overlap):
    overlap()
    x = x_ref[0]
    xb = x.astype(BF16)
    gs = jax.nn.sigmoid(_dot(xb, wgs_ref[...]))
    ga = jax.nn.sigmoid(_dot(xb, wga_ref[...]))
    mixed = gs * _dot(ys_ref[0], wbs_ref[...]) + ga * _dot(ya_ref[0], wba_ref[...])
    t = ALPHA * x + _dot(mixed.astype(BF16), wo_ref[...])
    o_ref[0] = _layer_norm(t, g_ref[...], b_ref[...])


def _mix(x, ys, ya, wgs, wga, wbs, wba, wo, g, b, tm, job=None):
    bsz, seq, _ = x.shape
    ws = (wgs, wga, wbs, wba, wo, g, b)
    in_specs = [
        pl.BlockSpec((1, tm, D_MODEL), lambda i, l, *_: (i, l, 0)),
        pl.BlockSpec((1, tm, D_SSM), lambda i, l, *_: (i, l, 0)),
        pl.BlockSpec((1, tm, D_ATT), lambda i, l, *_: (i, l, 0)),
    ] + [_const_spec(w.shape) for w in ws]
    return _tile_call(
        _mix_kernel, grid=(bsz, seq // tm), in_specs=in_specs,
        out_specs=(pl.BlockSpec((1, tm, D_MODEL), lambda i, l, *_: (i, l, 0)),),
        out_shape=(jax.ShapeDtypeStruct((bsz, seq, D_MODEL), F32),),
        scratch_shapes=[], args=(x, ys, ya, *ws), name="mix_ln1", job=job)


def _ffn_tail(x, a, am1, am2, bgate, cw_ref, cb_ref, wdn_ref, g_ref, b_ref, wpg_ref, bpg_ref, wple_ref, p):
    c = cb_ref[...] + cw_ref[0:1, :] * am2 + cw_ref[1:2, :] * am1 + cw_ref[2:3, :] * a
    hcat = (jax.nn.gelu(c) * bgate).astype(BF16)
    x2 = _layer_norm(ALPHA * x + _dot(hcat, wdn_ref[...]), g_ref[...], b_ref[...])
    gate = jax.nn.sigmoid(_dot(x2.astype(BF16), wpg_ref[...]) + bpg_ref[...])
    return x2 + gate * _dot(p.astype(BF16), wple_ref[...])


def _ffn_prompt_kernel(x_ref, p_ref, wup_ref, cw_ref, cb_ref, wdn_ref, g_ref, b_ref, wpg_ref, bpg_ref,
                       wple_ref, o_ref, tail_ref, prev_ref, *, tm, overlap):
    @pl.when(pl.program_id(1) == 0)
    def _():
        prev_ref[...] = jnp.zeros_like(prev_ref)

    overlap()
    x = x_ref[0]
    up = _dot(x.astype(BF16), wup_ref[...])
    a = up[:, :D_FF]
    row = lax.broadcasted_iota(jnp.int32, (tm, D_FF), 0)
    p1 = jnp.broadcast_to(prev_ref[SUBLANES - 1:SUBLANES, :], (tm, D_FF))
    p2 = jnp.broadcast_to(prev_ref[SUBLANES - 2:SUBLANES - 1, :], (tm, D_FF))
    am1 = jnp.where(row == 0, p1, pltpu.roll(a, 1, 0))
    am2 = jnp.where(row == 0, p2, jnp.where(row == 1, p1, pltpu.roll(a, 2, 0)))
    tail = a[tm - SUBLANES:, :]
    prev_ref[...] = tail
    tail_ref[0] = tail
    o_ref[0] = _ffn_tail(x, a, am1, am2, up[:, D_FF:], cw_ref, cb_ref, wdn_ref, g_ref, b_ref,
                         wpg_ref, bpg_ref, wple_ref, p_ref[0])


def _ffn_sample_kernel(x_ref, p_ref, s0_ref, s1_ref, wup_ref, cw_ref, cb_ref, wdn_ref, g_ref, b_ref,
                       wpg_ref, bpg_ref, wple_ref, o_ref, a_ref):
    x = x_ref[...]
    up = _dot(x.astype(BF16), wup_ref[...])
    a = up[:, :D_FF]
    a_ref[...] = a
    o_ref[...] = _ffn_tail(x, a, s1_ref[...], s0_ref[...], up[:, D_FF:], cw_ref, cb_ref, wdn_ref, g_ref,
                           b_ref, wpg_ref, bpg_ref, wple_ref, p_ref[...])


def _ffn_prompt(x, p, ws, tm, job=None):
    bsz, seq, _ = x.shape
    in_specs = [
        pl.BlockSpec((1, tm, D_MODEL), lambda i, l, *_: (i, l, 0)),
        pl.BlockSpec((1, tm, p.shape[-1]), lambda i, l, *_: (i, l, 0)),
    ] + [_resident_spec(w.shape) for w in ws]
    return _tile_call(
        functools.partial(_ffn_prompt_kernel, tm=tm), grid=(bsz, seq // tm), in_specs=in_specs,
        out_specs=(pl.BlockSpec((1, tm, D_MODEL), lambda i, l, *_: (i, l, 0)),
                   pl.BlockSpec((1, SUBLANES, D_FF), lambda i, l, *_: (i, 0, 0))),
        out_shape=(jax.ShapeDtypeStruct((bsz, seq, D_MODEL), F32),
                   jax.ShapeDtypeStruct((bsz, SUBLANES, D_FF), F32)),
        scratch_shapes=[pltpu.VMEM((SUBLANES, D_FF), F32)],
        args=(x, p, *ws), name="ffn_prompt", job=job)


def _ffn_sample(x, p, s0, s1, ws):
    n = x.shape[0]
    args = (x, p, s0, s1) + tuple(ws)
    return pl.pallas_call(
        _ffn_sample_kernel, grid=(1,), in_specs=[_const_spec(a.shape) for a in args],
        out_specs=(_const_spec((n, D_MODEL)), _const_spec((n, D_FF))),
        out_shape=(jax.ShapeDtypeStruct((n, D_MODEL), F32), jax.ShapeDtypeStruct((n, D_FF), F32)),
        compiler_params=_params(("arbitrary",)),
        name="ffn_sample",
    )(*args)


def _inproj_sample_kernel(x_ref, wu_ref, wt_ref, bfc_ref, u_ref, qt_ref, kt_ref, vt_ref, lft_ref):
    xb = x_ref[...].astype(BF16)
    u_ref[...] = _dot(xb, wu_ref[...])
    zt = _dot_nt(wt_ref[...], xb)
    qt_ref[...] = zt[0:D_ATT, :]
    kt_ref[...] = zt[D_ATT:2 * D_ATT, :]
    vt_ref[...] = zt[2 * D_ATT:3 * D_ATT, :]
    lft_ref[...] = _log_sigmoid(zt[3 * D_ATT:3 * D_ATT + N_HEADS, :] + bfc_ref[:, 0:1])


def _inproj_sample(x, wu, wt, bf_col):
    n = x.shape[0]
    args = (x, wu, wt, bf_col)
    shp = lambda r: jax.ShapeDtypeStruct((r, n), F32)
    return pl.pallas_call(
        _inproj_sample_kernel, grid=(1,), in_specs=[_const_spec(a.shape) for a in args],
        out_specs=(_const_spec((n, D_SSM)), _const_spec((D_ATT, n)), _const_spec((D_ATT, n)),
                   _const_spec((D_ATT, n)), _const_spec((N_HEADS, n))),
        out_shape=(jax.ShapeDtypeStruct((n, D_SSM), F32), shp(D_ATT), shp(D_ATT), shp(D_ATT), shp(N_HEADS)),
        compiler_params=_params(("arbitrary",)),
        name="inproj_sample",
    )(*args)


def _decode_only(job):
    def body(*, overlap):
        overlap()

    return _tile_call(body, grid=(job.count, job.steps_per_seq), in_specs=[], out_specs=(), out_shape=(),
                      scratch_shapes=[], args=(), name="decode_attn", job=job)[0]


def _discretize(a_re, a_im, log_dt, b_re, b_im):
    dt = jnp.exp(log_dt)[:, None]
    mag = jnp.exp(dt * a_re)
    ang = dt * a_im
    e_re = mag * jnp.cos(ang)
    e_im = mag * jnp.sin(ang)
    nr = e_re - 1.0
    ni = e_im
    den = a_re * a_re + a_im * a_im
    z_re = ((nr * a_re + ni * a_im) / den)[..., None]
    z_im = ((ni * a_re - nr * a_im) / den)[..., None]
    return e_re, e_im, z_re * b_re - z_im * b_im, z_re * b_im + z_im * b_re


def _block_diag(blocks):
    g, r, c = blocks.shape
    eye = jnp.eye(g, dtype=blocks.dtype)
    return (eye[:, None, :, None] * blocks[:, :, None, :]).reshape(g * r, g * c)


def _cmul(ar, ai, br, bi):
    return ar * br - ai * bi, ar * bi + ai * br


def _scan_tables(e_re, e_im):
    a1r, a1i = e_re.reshape(1, N_STATE), e_im.reshape(1, N_STATE)
    a2r, a2i = _cmul(a1r, a1i, a1r, a1i)
    a4r, a4i = _cmul(a2r, a2i, a2r, a2i)
    rows = jnp.arange(SUBLANES)[:, None]
    tabs_r, tabs_i = [], []
    for k, (pr, pi) in zip((1, 2, 4), ((a1r, a1i), (a2r, a2i), (a4r, a4i))):
        tabs_r.append(jnp.where(rows >= k, pr, 0.0))
        tabs_i.append(jnp.where(rows >= k, pi, 0.0))
    pw_r, pw_i = [a1r], [a1i]
    for _ in range(SUBLANES - 1):
        nr, ni = _cmul(pw_r[-1], pw_i[-1], a1r, a1i)
        pw_r.append(nr)
        pw_i.append(ni)
    return jnp.stack(tabs_r + tabs_i + [jnp.concatenate(pw_r, 0), jnp.concatenate(pw_i, 0)])


def kernel(x_prompt, x_sample, p_prompt, p_sample, cache_k, cache_v, cache_logf, page_table, state_ssm_re, state_ssm_im, state_conv, w_in, b_f, ssm_a_re, ssm_a_im, ssm_log_dt, ssm_b_re, ssm_b_im, ssm_c_re, ssm_c_im, ssm_d, w_glu, b_glu, w_br_ssm, w_br_att, w_o, ln1_g, ln1_b, w_up, conv_w, conv_b, w_down, ln2_g, ln2_b, w_ple_gate, b_ple_gate, w_ple):
    assert w_in.shape[0] == DEPTH
    bsz, seq, _ = x_prompt.shape
    nb = x_sample.shape[0]
    tm = 256

    win = w_in[0]
    o = [0, D_SSM, D_SSM + D_ATT, D_SSM + 2 * D_ATT, D_SSM + 3 * D_ATT, D_SSM + 3 * D_ATT + N_HEADS]
    wu, wq, wk, wv, wf = (win[:, o[i]:o[i + 1]] for i in range(5))
    wgs = win[:, o[5]:o[5] + D_MODEL].astype(BF16)
    wga = win[:, o[5] + D_MODEL:].astype(BF16)
    scale = HEAD_DIM ** -0.5
    wk_pad = jnp.pad(wk.reshape(D_MODEL, N_HEADS, HEAD_DIM), ((0, 0), (0, 0), (0, KAUG - HEAD_DIM)))
    wf_pad = jnp.pad(wf, ((0, 0), (0, LANES - N_HEADS)))
    w1 = jnp.concatenate([wu, wk_pad.reshape(D_MODEL, N_HEADS * KAUG), wf_pad], axis=1).astype(BF16)
    wt = jnp.concatenate([wq * scale, wk, wv, wf], axis=1).T.astype(BF16)
    wu_b = wu.astype(BF16)
    bf_row = jnp.pad(b_f, ((0, 0), (0, LANES - N_HEADS)))
    bf_col = jnp.broadcast_to(b_f[0][:, None], (N_HEADS, LANES))
    tri = jnp.tri(tm, dtype=BF16)
    prow = jnp.arange(3 * LANES)[:, None]
    pcol = jnp.arange(N_HEADS * KAUG)[None, :]
    place = jnp.where((prow % LANES < N_HEADS) & (pcol == (prow % LANES) * KAUG + HEAD_DIM + prow // LANES),
                      1.0, 0.0).astype(BF16)

    e_re, e_im, bb_re, bb_im = _discretize(ssm_a_re[0], ssm_a_im[0], ssm_log_dt[0], ssm_b_re[0], ssm_b_im[0])
    gpc = N_GROUPS // SSM_CHUNKS
    chunks = [slice(j * gpc, (j + 1) * gpc) for j in range(SSM_CHUNKS)]
    bt_re, bt_im = jnp.swapaxes(bb_re, 1, 2), jnp.swapaxes(bb_im, 1, 2)
    ct_re, ct_im = jnp.swapaxes(ssm_c_re[0], 1, 2), jnp.swapaxes(-ssm_c_im[0], 1, 2)
    bmat = jnp.stack([jnp.concatenate([_block_diag(bt_re[c]), _block_diag(bt_im[c])], axis=1)
                      for c in chunks]).astype(BF16)
    cmat = jnp.stack([jnp.concatenate([_block_diag(ct_re[c]), _block_diag(ct_im[c])], axis=0)
                      for c in chunks]).astype(BF16)
    tab = _scan_tables(e_re, e_im)
    a_re_row, a_im_row = e_re.reshape(1, N_STATE), e_im.reshape(1, N_STATE)
    wglu = w_glu[0].astype(BF16)
    mix_ws = (wgs, wga, w_br_ssm[0].astype(BF16), w_br_att[0].astype(BF16), w_o[0].astype(BF16), ln1_g, ln1_b)
    ffn_ws = (w_up[0].astype(BF16), conv_w[0], conv_b, w_down[0].astype(BF16), ln2_g, ln2_b,
              w_ple_gate[0].astype(BF16), b_ple_gate, w_ple[0].astype(BF16))

    xs = x_sample[:, 0]
    us, qts, kts, vts, lfts = _inproj_sample(xs, wu_b, wt, bf_col)
    kc = jnp.transpose(cache_k[0], (0, 2, 3, 1))
    vc = jnp.transpose(cache_v[0], (0, 2, 3, 1))
    lfc = jnp.transpose(cache_logf[0], (0, 2, 1))
    shp3 = (N_HEADS, HEAD_DIM, nb)
    todo = _DecodeJob(page_table.reshape(-1), kc, vc, lfc, qts.reshape(shp3), kts.reshape(shp3),
                      vts.reshape(shp3), lfts, 0, nb)
    host_steps = bsz * (seq // tm)
    decoded = []

    def hosted(call, *args, **kwargs):
        nonlocal todo
        count = host_steps // todo.steps_per_seq
        if count * todo.steps_per_seq != host_steps or count > todo.count:
            return call(*args, **kwargs)
        *outs, dec = call(*args, **kwargs, job=todo._replace(count=count))
        decoded.append(dec)
        todo = todo._replace(first=todo.first + count, count=todo.count - count)
        return outs

    u, kaug, qaug, kt, vt, vtb, lft = _inproj_prompt(x_prompt, w1, wt, bf_row, bf_col, tri, place, tm)
    ys, hre, him = hosted(_ssm_prompt, u, bmat, cmat, tab, ssm_d, wglu, b_glu, ts=tm)
    ya, = hosted(_attn_prompt, qaug, kaug, vtb, tq=512, heads=4)
    x1, = hosted(_mix, x_prompt, ys, ya, *mix_ws, tm=tm)
    y_prompt, tail = hosted(_ffn_prompt, x1, p_prompt[0], ffn_ws, tm=tm)
    if todo.count:
        decoded.append(_decode_only(todo))

    k_prompt = jnp.transpose(kt.reshape(bsz, N_HEADS, HEAD_DIM, seq), (0, 3, 1, 2))[None]
    v_prompt = jnp.transpose(vt.reshape(bsz, N_HEADS, HEAD_DIM, seq), (0, 3, 1, 2))[None]
    logf_prompt = jnp.transpose(lft, (0, 2, 1))[None]
    ssm_re_prompt = hre[:, 0].reshape(1, bsz, N_GROUPS, STATE_DIM)
    ssm_im_prompt = him[:, 0].reshape(1, bsz, N_GROUPS, STATE_DIM)
    conv_prompt = tail[:, SUBLANES - (CONV_W - 1):][None]

    ys_s, hr_s, hi_s = _ssm_sample(us, state_ssm_re[0].reshape(nb, N_STATE), state_ssm_im[0].reshape(nb, N_STATE),
                                   a_re_row, a_im_row, bmat, cmat, ssm_d, wglu, b_glu)
    yat = functools.reduce(lambda x, y: x + y, decoded)
    ya_s = yat.reshape(D_ATT, nb).T.astype(BF16)
    x1_s = _mix(xs[None], ys_s[None], ya_s[None], *mix_ws, tm=nb)[0][0]
    y_s, a_s = _ffn_sample(x1_s, p_sample[0, :, 0], state_conv[0, :, 0], state_conv[0, :, 1], ffn_ws)

    y_sample = y_s[:, None, :]
    k_sample = kts.T.reshape(1, nb, 1, N_HEADS, HEAD_DIM)
    v_sample = vts.T.reshape(1, nb, 1, N_HEADS, HEAD_DIM)
    logf_sample = lfts.T.reshape(1, nb, 1, N_HEADS)
    ssm_re_sample = hr_s.reshape(1, nb, N_GROUPS, STATE_DIM)
    ssm_im_sample = hi_s.reshape(1, nb, N_GROUPS, STATE_DIM)
    conv_sample = jnp.stack([state_conv[0, :, 1], a_s], axis=1)[None]

    return (y_prompt, y_sample, k_prompt, v_prompt, logf_prompt, ssm_re_prompt, ssm_im_prompt, conv_prompt,
            k_sample, v_sample, logf_sample, ssm_re_sample, ssm_im_sample, conv_sample)
```
